```python
import math
import jax
import jax.numpy as jnp
from jax import lax
import numpy as np


D_MODEL = 1024
BATCH = 2
SEQ = 16384
DEPTH = 4
DEC_BATCH = 16
DEC_SEQ = 4096
PAST_LEN = 128

N_MIXERS = 2
N_A_LAYERS = (DEPTH + 1) // 2
N_B_LAYERS = DEPTH // 2
HEAD_DIM = 64
N_HEADS = D_MODEL // HEAD_DIM
DIL_GROUPS = ((128, 1), (512, 4), (2048, 16))
N_GROUPS = len(DIL_GROUPS)
GROUP_WIDTH = N_HEADS * HEAD_DIM
QKV_WIDTH = 3 * N_GROUPS * GROUP_WIDTH
Q_BLOCK = 64
N_BUCKETS = 32
MAX_DISTANCE = 1024
N_FGROUPS = 4
FGROUP = D_MODEL // N_FGROUPS
D_FF = -((-8 * D_MODEL) // (3 * 256)) * 256
ALPHA = (2 * DEPTH) ** 0.25
BETA = (8 * DEPTH) ** -0.25
LN_EPS = 1e-5
NEG_INF = -1e30

kernel_name = 'hybrid_dilated_attn_fnet_encoder'


def layer_norm(x, g, b):
    xf = x.astype(jnp.float32)
    mu = jnp.mean(xf, axis=-1, keepdims=True)
    var = jnp.mean(jnp.square(xf - mu), axis=-1, keepdims=True)
    return ((xf - mu) * lax.rsqrt(var + LN_EPS) * g.astype(jnp.float32) + b.astype(jnp.float32)).astype(x.dtype)


def rel_bucket(rel):
    nb = N_BUCKETS // 2
    max_exact = nb // 2
    ret = jnp.where(rel > 0, nb, 0)
    n = jnp.abs(rel)
    nf = jnp.maximum(n, 1).astype(jnp.float32)
    large = max_exact + (jnp.log(nf / max_exact) / math.log(MAX_DISTANCE / max_exact) * (nb - max_exact)).astype(jnp.int32)
    large = jnp.minimum(large, nb - 1)
    return ret + jnp.where(n < max_exact, n, large)


def dilated_attention(q, k, v, bias_tbl, window, dilation):
    S, H, Dh = q.shape
    r = dilation
    half = window // (2 * r)
    kw = Q_BLOCK + 2 * half
    unit = r * Q_BLOCK
    s_pad = -(-S // unit) * unit
    L = s_pad // r
    nblk = L // Q_BLOCK

    def to_sub(t):
        t = jnp.pad(t, ((0, s_pad - S), (0, 0), (0, 0)))
        return t.reshape(L, r, H, Dh).transpose(1, 0, 2, 3)

    qb = to_sub(q).reshape(r, nblk, Q_BLOCK, H, Dh)
    kidx = jnp.arange(nblk)[:, None] * Q_BLOCK + jnp.arange(kw)[None, :]
    kb = jnp.pad(to_sub(k), ((0, 0), (half, half), (0, 0), (0, 0)))[:, kidx]
    vb = jnp.pad(to_sub(v), ((0, 0), (half, half), (0, 0), (0, 0)))[:, kidx]

    j = kidx - half
    pos = j[None] * r + jnp.arange(r)[:, None, None]
    key_ok = (j >= 0)[None] & (pos < S)
    rel = jnp.arange(kw)[None, :] - half - jnp.arange(Q_BLOCK)[:, None]
    band = jnp.abs(rel) <= half
    bias = jnp.moveaxis(bias_tbl[rel_bucket(rel * r)].astype(jnp.float32), -1, 0)

    s = jnp.einsum('rnqhd,rnkhd->rnhqk', qb, kb, preferred_element_type=jnp.float32) * (HEAD_DIM ** -0.5) + bias
    ok = band[None, None, None] & key_ok[:, :, None, None, :]
    s = jnp.where(ok, s, NEG_INF)
    m = jnp.max(s, axis=-1, keepdims=True)
    p = jnp.exp(s - m)
    den = jnp.sum(p, axis=-1, keepdims=True)
    o = jnp.einsum('rnhqk,rnkhd->rnqhd', p, vb.astype(jnp.float32)) / den.transpose(0, 1, 3, 2, 4)
    lse = (m + jnp.log(den))[..., 0].transpose(0, 1, 3, 2)

    o = o.reshape(r, L, H, Dh).transpose(1, 0, 2, 3).reshape(s_pad, H, Dh)[:S]
    lse = lse.reshape(r, L, H).transpose(1, 0, 2).reshape(s_pad, H)[:S]
    return o, lse


def mixer_a_seq(x, w_qkv, w_o, rel_bias):
    S = x.shape[0]
    qkv = (x @ w_qkv).reshape(S, 3, N_GROUPS, N_HEADS, HEAD_DIM)
    outs = []
    lses = []
    for g, (window, dil) in enumerate(DIL_GROUPS):
        o, l = dilated_attention(qkv[:, 0, g], qkv[:, 1, g], qkv[:, 2, g],
                                 rel_bias[:, g * N_HEADS:(g + 1) * N_HEADS], window, dil)
        outs.append(o)
        lses.append(l)
    wts = jax.nn.softmax(jnp.stack(lses), axis=0)
    o = jnp.sum(wts[..., None] * jnp.stack(outs), axis=0)
    return o.reshape(S, GROUP_WIDTH).astype(x.dtype) @ w_o


def mixer_a(x, w_qkv, w_o, rel_bias):
    return lax.map(lambda xs: mixer_a_seq(xs, w_qkv, w_o, rel_bias), x)


def mixer_b(x, w_o, b_o):
    B, S, D = x.shape
    xg = x.astype(jnp.float32).reshape(B, S, N_FGROUPS, FGROUP)
    f = jnp.fft.fft2(xg, axes=(1, 3), norm='ortho').real
    return f.reshape(B, S, D).astype(x.dtype) @ w_o + b_o


def swiglu(x, w_gate, w_up, w_down):
    return (jax.nn.silu(x @ w_gate) * (x @ w_up)) @ w_down


def encoder_trunk(x, rel_bias, w_qkv_a, w_o_a, w_o_b, b_o_b, w_gate, w_up, w_down, ln1_g, ln1_b, ln2_g, ln2_b):
    for i in range(DEPTH):
        li = i // N_MIXERS
        if i % N_MIXERS == 0:
            h = mixer_a(x, w_qkv_a[li], w_o_a[li], rel_bias)
        else:
            h = mixer_b(x, w_o_b[li], b_o_b[li])
        x = layer_norm(ALPHA * x + h, ln1_g[i], ln1_b[i])
        x = layer_norm(ALPHA * x + swiglu(x, w_gate[i], w_up[i], w_down[i]), ln2_g[i], ln2_b[i])
    return x


def setup_inputs(seed: int = 0) -> dict:
    key = jax.random.key(seed)
    ks = jax.random.split(key, 15)

    def nrm(k, shape, scale):
        return jax.random.normal(k, shape, jnp.float32) * scale

    return {
        'x_prompt': nrm(ks[0], (BATCH, SEQ, D_MODEL), 1.0),
        'x_sample': nrm(ks[1], (DEC_BATCH, DEC_SEQ, D_MODEL), 1.0),
        'rel_bias': nrm(ks[2], (N_BUCKETS, N_GROUPS * N_HEADS), 0.5),
        'w_qkv_a': nrm(ks[3], (N_A_LAYERS, D_MODEL, QKV_WIDTH), D_MODEL ** -0.5),
        'w_o_a': nrm(ks[4], (N_A_LAYERS, GROUP_WIDTH, D_MODEL), BETA * GROUP_WIDTH ** -0.5),
        'w_o_b': nrm(ks[5], (N_B_LAYERS, D_MODEL, D_MODEL), BETA * D_MODEL ** -0.5),
        'b_o_b': nrm(ks[6], (N_B_LAYERS, D_MODEL), 0.02),
        'w_gate': nrm(ks[7], (DEPTH, D_MODEL, D_FF), D_MODEL ** -0.5),
        'w_up': nrm(ks[8], (DEPTH, D_MODEL, D_FF), D_MODEL ** -0.5),
        'w_down': nrm(ks[9], (DEPTH, D_FF, D_MODEL), BETA * D_FF ** -0.5),
        'ln1_g': 1.0 + nrm(ks[10], (DEPTH, D_MODEL), 0.02),
        'ln1_b': nrm(ks[11], (DEPTH, D_MODEL), 0.02),
        'ln2_g': 1.0 + nrm(ks[12], (DEPTH, D_MODEL), 0.02),
        'ln2_b': nrm(ks[13], (DEPTH, D_MODEL), 0.02),
    }


def reference(x_prompt, x_sample, rel_bias, w_qkv_a, w_o_a, w_o_b, b_o_b, w_gate, w_up, w_down, ln1_g, ln1_b, ln2_g, ln2_b):
    y_prompt = encoder_trunk(x_prompt, rel_bias, w_qkv_a, w_o_a, w_o_b, b_o_b, w_gate, w_up, w_down, ln1_g, ln1_b, ln2_g, ln2_b)
    y_sample = encoder_trunk(x_sample, rel_bias, w_qkv_a, w_o_a, w_o_b, b_o_b, w_gate, w_up, w_down, ln1_g, ln1_b, ln2_g, ln2_b)
    return (y_prompt, y_sample)
```

```python
import functools
import math

import numpy as np
import jax
import jax.numpy as jnp
from jax import lax
from jax.experimental import pallas as pl
from jax.experimental.pallas import tpu as pltpu

D_MODEL = 1024
DEPTH = 4
HEAD_DIM = 64
N_HEADS = 16
DIL_GROUPS = ((128, 1), (512, 4), (2048, 16))
N_GROUPS = len(DIL_GROUPS)
QKV_WIDTH = 3 * N_GROUPS * D_MODEL
N_BUCKETS = 32
MAX_DISTANCE = 1024
N_FGROUPS = 4
FGROUP = D_MODEL // N_FGROUPS
D_FF = 2816
ALPHA = (2 * DEPTH) ** 0.25
LN_EPS = 1e-5
NEG_INF = -1e30

HALF = 64
QT = 128
KW = QT + 2 * HALF
LANES = 128
MXU_DIM = 256
FF_CHUNK = MXU_DIM
VMEM_LIMIT = 52 * 2 ** 20

BF16 = jnp.bfloat16
F32 = jnp.float32


def _params(*sem):
    return pltpu.CompilerParams(dimension_semantics=sem, vmem_limit_bytes=VMEM_LIMIT)


def _const_spec(shape):
    nd = len(shape)
    return pl.BlockSpec(shape, lambda *_: (0,) * nd, pipeline_mode=pl.Buffered(1))


def _layer_norm(y, g, b):
    mu = jnp.mean(y, axis=-1, keepdims=True)
    yc = y - mu
    var = jnp.mean(yc * yc, axis=-1, keepdims=True)
    return yc * lax.rsqrt(var + LN_EPS) * g + b


def _qkv_body(x_ref, w_ref, o_ref):
    o_ref[...] = jnp.dot(x_ref[...].astype(BF16), w_ref[...],
                         preferred_element_type=F32).astype(o_ref.dtype)


def _qkv_proj(x, w):
    m = x.shape[0]
    bm = min(1024, m)
    bn = 1024
    return pl.pallas_call(
        _qkv_body,
        grid=(m // bm, QKV_WIDTH // bn),
        in_specs=[pl.BlockSpec((bm, D_MODEL), lambda i, j: (i, 0)),
                  pl.BlockSpec((D_MODEL, bn), lambda i, j: (0, j))],
        out_specs=pl.BlockSpec((bm, bn), lambda i, j: (i, j)),
        out_shape=jax.ShapeDtypeStruct((m, QKV_WIDTH), BF16),
        compiler_params=_params("parallel", "arbitrary"),
        name="qkv_proj",
    )(x, w)


def _attn_body(q_ref, kp_ref, kc_ref, kn_ref, vp_ref, vc_ref, vn_ref, bias_ref,
               o_ref, lse_ref, kbuf, vbuf, *, tile, sub_len):
    i = pl.program_id(2)
    kbuf[0:HALF, :] = kp_ref[...]
    kbuf[HALF:HALF + tile, :] = kc_ref[...]
    kbuf[HALF + tile:, :] = kn_ref[...]
    vbuf[0:HALF, :] = vp_ref[...]
    vbuf[HALF:HALF + tile, :] = vc_ref[...]
    vbuf[HALF + tile:, :] = vn_ref[...]

    first_head = lax.broadcasted_iota(jnp.int32, (QT, LANES), 1) < HEAD_DIM
    key_off = lax.broadcasted_iota(jnp.int32, (1, KW), 1) - HALF

    def sub_tile(t, carry):
        r0 = pl.multiple_of(t * QT, QT)
        key_idx = i * tile + t * QT + key_off
        key_mask = jnp.where((key_idx >= 0) & (key_idx < sub_len), 0.0, NEG_INF)
        for hp in range(N_HEADS // 2):
            cols = slice(hp * LANES, (hp + 1) * LANES)
            q2 = q_ref[pl.ds(r0, QT), cols]
            kw = kbuf[pl.ds(r0, KW), cols]
            vw = vbuf[pl.ds(r0, KW), cols]
            outs, lses = [], []
            for hh in range(2):
                qm = jnp.where(first_head if hh == 0 else ~first_head, q2, jnp.zeros_like(q2))
                s = lax.dot_general(qm, kw, (((1,), (1,)), ((), ())),
                                    preferred_element_type=F32)
                s = s + bias_ref[2 * hp + hh] + key_mask
                m = jnp.max(s, axis=-1, keepdims=True)
                p = jnp.exp(s - m)
                den = jnp.sum(p, axis=-1, keepdims=True)
                pv = jnp.dot(p.astype(BF16), vw, preferred_element_type=F32)
                outs.append(pv * (1.0 / den))
                lses.append(jnp.broadcast_to(m + jnp.log(den), (QT, LANES)))
            o_ref[pl.ds(r0, QT), cols] = jnp.where(first_head, outs[0], outs[1]).astype(BF16)
            lse_ref[pl.ds(r0, QT), cols] = jnp.where(first_head, lses[0], lses[1])
        return carry

    lax.fori_loop(0, tile // QT, sub_tile, 0)


def _attention_group(qkv, bias, g, r, n_seq, seq_len):
    m = n_seq * seq_len
    sub_len = seq_len // r
    tile = min(256, sub_len)
    assert sub_len % tile == 0 and tile % QT == 0
    n_tiles = sub_len // tile
    hb = tile // HALF
    last_hb = m // r // HALF - 1
    qkv_v = qkv.reshape(m // r, r * QKV_WIDTH)
    n_col = QKV_WIDTH // D_MODEL

    def cur(which):
        return pl.BlockSpec((tile, D_MODEL),
                            lambda b, c, i: (b * n_tiles + i, c * n_col + which * N_GROUPS + g))

    def prev(which):
        return pl.BlockSpec((HALF, D_MODEL),
                            lambda b, c, i: (jnp.maximum((b * n_tiles + i) * hb - 1, 0),
                                             c * n_col + which * N_GROUPS + g))

    def nxt(which):
        return pl.BlockSpec((HALF, D_MODEL),
                            lambda b, c, i: (jnp.minimum((b * n_tiles + i + 1) * hb, last_hb),
                                             c * n_col + which * N_GROUPS + g))

    out_spec = pl.BlockSpec((tile, D_MODEL), lambda b, c, i: (b * n_tiles + i, c))
    o, lse = pl.pallas_call(
        functools.partial(_attn_body, tile=tile, sub_len=sub_len),
        grid=(n_seq, r, n_tiles),
        in_specs=[cur(0), prev(1), cur(1), nxt(1), prev(2), cur(2), nxt(2),
                  _const_spec((N_HEADS, QT, KW))],
        out_specs=[out_spec, out_spec],
        out_shape=[jax.ShapeDtypeStruct((m // r, r * D_MODEL), BF16),
                   jax.ShapeDtypeStruct((m // r, r * D_MODEL), F32)],
        scratch_shapes=[pltpu.VMEM((tile + 2 * HALF, D_MODEL), BF16),
                        pltpu.VMEM((tile + 2 * HALF, D_MODEL), BF16)],
        compiler_params=_params("parallel", "parallel", "arbitrary"),
        name=f"attn_r{r}",
    )(qkv_v, qkv_v, qkv_v, qkv_v, qkv_v, qkv_v, qkv_v, bias)
    return o.reshape(m, D_MODEL), lse.reshape(m, D_MODEL)


def _rel_bucket_np(rel):
    nb = N_BUCKETS // 2
    max_exact = nb // 2
    ret = np.where(rel > 0, nb, 0)
    n = np.abs(rel)
    nf = np.maximum(n, 1).astype(np.float32)
    large = max_exact + (np.log(nf / np.float32(max_exact)) / np.float32(math.log(MAX_DISTANCE / max_exact))
                         * np.float32(nb - max_exact)).astype(np.int32)
    large = np.minimum(large, nb - 1)
    return ret + np.where(n < max_exact, n, large)


def _band_bias(rel_bias):
    rel = np.arange(KW)[None, :] - HALF - np.arange(QT)[:, None]
    band = np.abs(rel) <= HALF
    tables = []
    for g, (_, r) in enumerate(DIL_GROUPS):
        bucket = _rel_bucket_np(np.where(band, rel, 0) * r)
        tbl = rel_bias[:, g * N_HEADS:(g + 1) * N_HEADS][bucket]
        tbl = jnp.where(band[:, :, None], tbl.astype(F32), NEG_INF)
        tables.append(jnp.moveaxis(tbl, -1, 0))
    return jnp.stack(tables)


def _merge_proj_body(o0_ref, o1_ref, o2_ref, l0_ref, l1_ref, l2_ref, x_ref, w_ref, g_ref, b_ref, out_ref):
    l0, l1, l2 = l0_ref[...], l1_ref[...], l2_ref[...]
    top = jnp.maximum(jnp.maximum(l0, l1), l2)
    e0, e1, e2 = jnp.exp(l0 - top), jnp.exp(l1 - top), jnp.exp(l2 - top)
    o = (e0 * o0_ref[...].astype(F32) + e1 * o1_ref[...].astype(F32)
         + e2 * o2_ref[...].astype(F32)) * (1.0 / (e0 + e1 + e2))
    h = jnp.dot(o.astype(BF16), w_ref[...], preferred_element_type=F32)
    out_ref[...] = _layer_norm(ALPHA * x_ref[...] + h, g_ref[...], b_ref[...])


def _merge_proj(outs, lses, x, w, g, b):
    m = x.shape[0]
    bm = min(256, m)
    row = pl.BlockSpec((bm, D_MODEL), lambda i: (i, 0))
    return pl.pallas_call(
        _merge_proj_body,
        grid=(m // bm,),
        in_specs=[row] * 7 + [_const_spec((D_MODEL, D_MODEL)), _const_spec((1, D_MODEL)),
                              _const_spec((1, D_MODEL))],
        out_specs=row,
        out_shape=jax.ShapeDtypeStruct((m, D_MODEL), F32),
        compiler_params=_params("parallel"),
        name="merge_proj_ln",
    )(*outs, *lses, x, w, g, b)


def _proj_bias_body(f_ref, x_ref, w_ref, bo_ref, g_ref, b_ref, out_ref):
    h = jnp.dot(f_ref[...], w_ref[...], preferred_element_type=F32) + bo_ref[...]
    out_ref[...] = _layer_norm(ALPHA * x_ref[...] + h, g_ref[...], b_ref[...])


def _proj_bias(f, x, w, bo, g, b):
    m = x.shape[0]
    bm = min(512, m)
    row = pl.BlockSpec((bm, D_MODEL), lambda i: (i, 0))
    vec = _const_spec((1, D_MODEL))
    return pl.pallas_call(
        _proj_bias_body,
        grid=(m // bm,),
        in_specs=[row, row, _const_spec((D_MODEL, D_MODEL)), vec, vec, vec],
        out_specs=row,
        out_shape=jax.ShapeDtypeStruct((m, D_MODEL), F32),
        compiler_params=_params("parallel"),
        name="proj_bias_ln",
    )(f, x, w, bo, g, b)


def _ffn_body(x_ref, wg_ref, wu_ref, wd_ref, g_ref, b_ref, out_ref):
    x = x_ref[...]
    xb = x.astype(BF16)
    acc = jnp.zeros(x.shape, F32)
    for c in range(D_FF // FF_CHUNK):
        cols = slice(c * FF_CHUNK, (c + 1) * FF_CHUNK)
        gate = jnp.dot(xb, wg_ref[:, cols], preferred_element_type=F32)
        up = jnp.dot(xb, wu_ref[:, cols], preferred_element_type=F32)
        h = (gate * jax.nn.sigmoid(gate) * up).astype(BF16)
        acc = acc + jnp.dot(h, wd_ref[cols, :], preferred_element_type=F32)
    out_ref[...] = _layer_norm(ALPHA * x + acc, g_ref[...], b_ref[...])


def _ffn(x, wg, wu, wd, g, b):
    m = x.shape[0]
    bm = min(512, m)
    row = pl.BlockSpec((bm, D_MODEL), lambda i: (i, 0))
    vec = _const_spec((1, D_MODEL))
    return pl.pallas_call(
        _ffn_body,
        grid=(m // bm,),
        in_specs=[row, _const_spec((D_MODEL, D_FF)), _const_spec((D_MODEL, D_FF)),
                  _const_spec((D_FF, D_MODEL)), vec, vec],
        out_specs=row,
        out_shape=jax.ShapeDtypeStruct((m, D_MODEL), F32),
        compiler_params=_params("parallel"),
        name="swiglu_ln",
    )(x, wg, wu, wd, g, b)


def _dft_factors(seq_len):
    bits = seq_len.bit_length() - 1
    assert 1 << bits == seq_len
    n1 = 1 << ((bits + 1) // 2)
    return n1, seq_len // n1


def _dft_tables(seq_len):
    n1, n2 = _dft_factors(seq_len)
    c = np.arange(FGROUP)
    ang = 2 * np.pi * ((c[:, None] * c[None, :]) % FGROUP) / FGROUP
    wc = np.concatenate([np.cos(ang), np.sin(ang)], axis=1) / math.sqrt(FGROUP)
    k1 = np.arange(n1)
    ang = 2 * np.pi * ((k1[:, None] * k1[None, :]) % n1) / n1
    w1 = np.block([[np.cos(ang), -np.sin(ang)], [np.sin(ang), np.cos(ang)]])
    k = k1[:, None, None] + n1 * np.arange(n2)[None, :, None]
    ang = 2 * np.pi * ((k * np.arange(n2)[None, None, :]) % seq_len) / seq_len
    m2c = np.cos(ang) / math.sqrt(seq_len)
    m2s = -np.sin(ang) / math.sqrt(seq_len)
    return tuple(jnp.asarray(t, dtype=BF16) for t in (wc, w1, m2c, m2s))


def _dft_chan_body(x_ref, wc_ref, z_ref):
    xb = x_ref[...].astype(BF16)
    for g in range(N_FGROUPS):
        cols = slice(g * FGROUP, (g + 1) * FGROUP)
        y = jnp.dot(xb[:, cols], wc_ref[...], preferred_element_type=F32)
        z_ref[0, :, cols] = y[:, :FGROUP].astype(BF16)
        z_ref[1, :, cols] = y[:, FGROUP:].astype(BF16)


def _dft_stage1_body(z_ref, w_ref, ar_ref, ai_ref, *, n1):
    a = jnp.dot(w_ref[...], z_ref[...], preferred_element_type=F32)
    ar_ref[...] = a[:n1].astype(BF16)
    ai_ref[...] = a[n1:].astype(BF16)


def _dft_stage2_body(ar_ref, ai_ref, mc_ref, ms_ref, f_ref, *, kb):
    for q in range(kb):
        f = (jnp.dot(mc_ref[q], ar_ref[q], preferred_element_type=F32)
             + jnp.dot(ms_ref[q], ai_ref[q], preferred_element_type=F32))
        f_ref[:, q * D_MODEL:(q + 1) * D_MODEL] = f.astype(BF16)


def _fourier_real(x, n_seq, seq_len):
    n1, n2 = _dft_factors(seq_len)
    wc, w1, m2c, m2s = _dft_tables(seq_len)
    bm = min(512, seq_len)
    z = pl.pallas_call(
        _dft_chan_body,
        grid=(n_seq, seq_len // bm),
        in_specs=[pl.BlockSpec((None, bm, D_MODEL), lambda b, i: (b, i, 0)),
                  _const_spec((FGROUP, 2 * FGROUP))],
        out_specs=pl.BlockSpec((None, 2, bm, D_MODEL), lambda b, i: (b, 0, i, 0)),
        out_shape=jax.ShapeDtypeStruct((n_seq, 2, seq_len, D_MODEL), BF16),
        compiler_params=_params("parallel", "parallel"),
        name="dft_channel",
    )(x.reshape(n_seq, seq_len, D_MODEL), wc)
    width = n2 * D_MODEL
    bn = min(4096, width)
    part = pl.BlockSpec((None, n1, bn), lambda b, j: (b, 0, j))
    ar, ai = pl.pallas_call(
        functools.partial(_dft_stage1_body, n1=n1),
        grid=(n_seq, width // bn),
        in_specs=[pl.BlockSpec((None, 2 * n1, bn), lambda b, j: (b, 0, j)),
                  _const_spec((2 * n1, 2 * n1))],
        out_specs=[part, part],
        out_shape=[jax.ShapeDtypeStruct((n_seq, n1, width), BF16)] * 2,
        compiler_params=_params("parallel", "parallel"),
        name="dft_stage1",
    )(z.reshape(n_seq, 2 * n1, width), w1)
    kb = min(8, n1)
    a_spec = pl.BlockSpec((None, kb, n2, D_MODEL), lambda b, k: (b, k, 0, 0))
    m_spec = pl.BlockSpec((kb, n2, n2), lambda b, k: (k, 0, 0))
    f = pl.pallas_call(
        functools.partial(_dft_stage2_body, kb=kb),
        grid=(n_seq, n1 // kb),
        in_specs=[a_spec, a_spec, m_spec, m_spec],
        out_specs=pl.BlockSpec((None, n2, kb * D_MODEL), lambda b, k: (b, 0, k)),
        out_shape=jax.ShapeDtypeStruct((n_seq, n2, n1 * D_MODEL), BF16),
        compiler_params=_params("parallel", "parallel"),
        name="dft_stage2",
    )(ar.reshape(n_seq, n1, n2, D_MODEL), ai.reshape(n_seq, n1, n2, D_MODEL), m2c, m2s)
    return f.reshape(n_seq * seq_len, D_MODEL)


def _trunk(x, p):
    n_seq, seq_len, _ = x.shape
    x = x.reshape(n_seq * seq_len, D_MODEL)
    for i in range(DEPTH):
        li = i // 2
        ln1 = (p["ln1_g"][i], p["ln1_b"][i])
        if i % 2 == 0:
            qkv = _qkv_proj(x, p["w_qkv"][li])
            outs, lses = [], []
            for g, (_, r) in enumerate(DIL_GROUPS):
                o, lse = _attention_group(qkv, p["band_bias"][g], g, r, n_seq, seq_len)
                outs.append(o)
                lses.append(lse)
            x = _merge_proj(outs, lses, x, p["w_o_a"][li], *ln1)
        else:
            f = _fourier_real(x, n_seq, seq_len)
            x = _proj_bias(f, x, p["w_o_b"][li], p["b_o_b"][li], *ln1)
        x = _ffn(x, p["w_gate"][i], p["w_up"][i], p["w_down"][i], p["ln2_g"][i], p["ln2_b"][i])
    return x.reshape(n_seq, seq_len, D_MODEL)


def kernel(x_prompt, x_sample, rel_bias, w_qkv_a, w_o_a, w_o_b, b_o_b, w_gate, w_up, w_down,
           ln1_g, ln1_b, ln2_g, ln2_b):
    q_scale = jnp.concatenate([jnp.full((QKV_WIDTH // 3,), HEAD_DIM ** -0.5, F32),
                               jnp.ones((2 * QKV_WIDTH // 3,), F32)])
    vec = lambda a: a.astype(F32)[:, None, :]
    p = {
        "w_qkv": (w_qkv_a * q_scale).astype(BF16),
        "w_o_a": w_o_a.astype(BF16),
        "w_o_b": w_o_b.astype(BF16),
        "b_o_b": vec(b_o_b),
        "w_gate": w_gate.astype(BF16),
        "w_up": w_up.astype(BF16),
        "w_down": w_down.astype(BF16),
        "ln1_g": vec(ln1_g), "ln1_b": vec(ln1_b), "ln2_g": vec(ln2_g), "ln2_b": vec(ln2_b),
        "band_bias": _band_bias(rel_bias),
    }
    return _trunk(x_prompt, p), _trunk(x_sample, p)
```

```python
import functools
import math

import numpy as np
import jax
import jax.numpy as jnp
from jax import lax
from jax.experimental import pallas as pl
from jax.experimental.pallas import tpu as pltpu

D_MODEL = 1024
DEPTH = 4
HEAD_DIM = 64
N_HEADS = 16
DIL_GROUPS = ((128, 1), (512, 4), (2048, 16))
N_GROUPS = len(DIL_GROUPS)
N_BUCKETS = 32
MAX_DISTANCE = 1024
N_FGROUPS = 4
FGROUP = D_MODEL // N_FGROUPS
D_FF = 2816
ALPHA = (2 * DEPTH) ** 0.25
LN_EPS = 1e-5
NEG_INF = -1e30
LOG2E = math.log2(math.e)
LN2 = math.log(2.0)

HALF = 64
QT = 128
KW = QT + 2 * HALF
LANES = 128
MXU_DIM = 256
FF_CHUNK = MXU_DIM
FF_ROWS = 512
DFT_SLAB = 8
VMEM_LIMIT = 52 * 2 ** 20

BF16 = jnp.bfloat16
F32 = jnp.float32
U32 = jnp.uint32


def _params(*sem):
    return pltpu.CompilerParams(dimension_semantics=sem, vmem_limit_bytes=VMEM_LIMIT)


def _const_spec(shape):
    nd = len(shape)
    return pl.BlockSpec(shape, lambda *_: (0,) * nd, pipeline_mode=pl.Buffered(1))


def _layer_norm(y, g, b):
    mu = jnp.mean(y, axis=-1, keepdims=True)
    yc = y - mu
    var = jnp.mean(yc * yc, axis=-1, keepdims=True)
    return yc * lax.rsqrt(var + LN_EPS) * g + b


def _lane_tile_scratch(rows, dtype):
    return pltpu.VMEM((D_MODEL // LANES, rows, LANES), dtype)


def _stage_lane_tiles(stage, v):
    for j in range(D_MODEL // LANES):
        stage[j] = v[:, j * LANES:(j + 1) * LANES]


def _strided_rows(stage, start, size, stride):
    return jnp.concatenate([stage[j, pl.ds(start, size, stride=stride), :]
                            for j in range(D_MODEL // LANES)], axis=1)


def _qkv_body(x_ref, w_ref, o_ref, xs, *stage, r, n):
    if r == 1:
        xs[...] = x_ref[...].astype(BF16)
    else:
        _stage_lane_tiles(stage[0], x_ref[...])
        for c in range(r):
            xs[c * n:(c + 1) * n, :] = _strided_rows(stage[0], c, n, r).astype(BF16)
    for j in range(3):
        cols = slice(j * D_MODEL, (j + 1) * D_MODEL)
        res = jnp.dot(xs[...], w_ref[:, cols], preferred_element_type=F32)
        o_ref[:, :, cols] = res.reshape(r, n, D_MODEL).astype(BF16)


def _qkv_group(x, w, r):
    n_seq, seq_len, _ = x.shape
    bm = 512
    n = bm // r
    return pl.pallas_call(
        functools.partial(_qkv_body, r=r, n=n),
        grid=(n_seq, seq_len // bm),
        in_specs=[pl.BlockSpec((None, bm, D_MODEL), lambda b, i: (b, i, 0)),
                  _const_spec((D_MODEL, 3 * D_MODEL))],
        out_specs=pl.BlockSpec((None, r, n, 3 * D_MODEL), lambda b, i: (b, 0, i, 0)),
        out_shape=jax.ShapeDtypeStruct((n_seq, r, seq_len // r, 3 * D_MODEL), BF16),
        scratch_shapes=[pltpu.VMEM((bm, D_MODEL), BF16)] + ([] if r == 1 else [_lane_tile_scratch(bm, F32)]),
        compiler_params=_params("parallel", "parallel"),
        name=f"qkv_r{r}",
    )(x, w)


def _attn_body(q_ref, kp_ref, kc_ref, kn_ref, vp_ref, vc_ref, vn_ref, bias_ref,
               o_ref, lse_ref, kbuf, vbuf, *, tile, n_tiles):
    i = pl.program_id(2)
    kbuf[0:HALF, :] = kp_ref[...]
    kbuf[HALF:HALF + tile, :] = kc_ref[...]
    kbuf[HALF + tile:, :] = kn_ref[...]
    vbuf[0:HALF, :] = vp_ref[...]
    vbuf[HALF:HALF + tile, :] = vc_ref[...]
    vbuf[HALF + tile:, :] = vn_ref[...]

    lane = lax.broadcasted_iota(jnp.int32, (QT, LANES), 1)
    row = lax.broadcasted_iota(jnp.int32, (QT, LANES), 0)
    first_head = lane < HEAD_DIM
    eye = jnp.where(lane == row, 1.0, 0.0).astype(BF16)
    n_sub = tile // QT

    def sub_tile(t, carry):
        r0 = pl.multiple_of(t * QT, QT)
        at_start = jnp.logical_and(i == 0, t == 0).astype(jnp.int32)
        at_end = jnp.logical_and(i == n_tiles - 1, t == n_sub - 1).astype(jnp.int32)
        variant = at_start + 2 * at_end
        lse_tile = jnp.zeros((QT, LANES), F32)
        for hp in range(N_HEADS // 2):
            cols = slice(hp * LANES, (hp + 1) * LANES)
            q2 = q_ref[pl.ds(r0, QT), cols]
            kw = kbuf[pl.ds(r0, KW), cols]
            vw = vbuf[pl.ds(r0, KW), cols]
            outs = []
            for hh in range(2):
                h = 2 * hp + hh
                qm = jnp.where(first_head if hh == 0 else ~first_head, q2, jnp.zeros_like(q2))
                lhs = jnp.concatenate([qm, eye], axis=1)
                rhs = jnp.concatenate([kw, bias_ref[variant, h]], axis=1)
                s = lax.dot_general(lhs, rhs, (((1,), (1,)), ((), ())),
                                    preferred_element_type=F32)
                m = jnp.max(s, axis=-1, keepdims=True)
                p = jnp.exp2(s - m)
                den = jnp.sum(p, axis=-1, keepdims=True)
                pv = jnp.dot(p.astype(BF16), vw, preferred_element_type=F32)
                outs.append(pv * (1.0 / den))
                lse_tile = jnp.where(lane == h, m * LN2 + jnp.log(den), lse_tile)
            o_ref[pl.ds(r0, QT), cols] = jnp.where(first_head, outs[0], outs[1]).astype(BF16)
        lse_ref[pl.ds(r0, QT), :] = lse_tile
        return carry

    lax.fori_loop(0, n_sub, sub_tile, 0)


def _attention_group(qkv, bias_t):
    n_seq, r, sub_len, _ = qkv.shape
    tile = min(512, sub_len)
    assert sub_len % tile == 0 and tile % QT == 0
    n_tiles = sub_len // tile
    hb = tile // HALF
    last_hb = sub_len // HALF - 1

    def cur(which):
        return pl.BlockSpec((None, None, tile, D_MODEL), lambda b, c, i: (b, c, i, which))

    def prev(which):
        return pl.BlockSpec((None, None, HALF, D_MODEL),
                            lambda b, c, i: (b, c, jnp.maximum(i * hb - 1, 0), which))

    def nxt(which):
        return pl.BlockSpec((None, None, HALF, D_MODEL),
                            lambda b, c, i: (b, c, jnp.minimum((i + 1) * hb, last_hb), which))

    return pl.pallas_call(
        functools.partial(_attn_body, tile=tile, n_tiles=n_tiles),
        grid=(n_seq, r, n_tiles),
        in_specs=[cur(0), prev(1), cur(1), nxt(1), prev(2), cur(2), nxt(2),
                  _const_spec((4, N_HEADS, KW, QT))],
        out_specs=[pl.BlockSpec((None, None, tile, D_MODEL), lambda b, c, i: (b, c, i, 0)),
                   pl.BlockSpec((None, None, tile, LANES), lambda b, c, i: (b, c, i, 0))],
        out_shape=[jax.ShapeDtypeStruct((n_seq, r, sub_len, D_MODEL), BF16),
                   jax.ShapeDtypeStruct((n_seq, r, sub_len, LANES), F32)],
        scratch_shapes=[pltpu.VMEM((tile + 2 * HALF, D_MODEL), BF16),
                        pltpu.VMEM((tile + 2 * HALF, D_MODEL), BF16)],
        compiler_params=_params("parallel", "parallel", "arbitrary"),
        name=f"attn_r{r}",
    )(qkv, qkv, qkv, qkv, qkv, qkv, qkv, bias_t)


def _rel_bucket_np(rel):
    nb = N_BUCKETS // 2
    max_exact = nb // 2
    ret = np.where(rel > 0, nb, 0)
    n = np.abs(rel)
    nf = np.maximum(n, 1).astype(np.float32)
    large = max_exact + (np.log(nf / np.float32(max_exact)) / np.float32(math.log(MAX_DISTANCE / max_exact))
                         * np.float32(nb - max_exact)).astype(np.int32)
    large = np.minimum(large, nb - 1)
    return ret + np.where(n < max_exact, n, large)


def _band_bias(rel_bias):
    kk = np.arange(KW)[None, :]
    tables = []
    for g, (_, r) in enumerate(DIL_GROUPS):
        offs = np.arange(-HALF, HALF + 1)
        vals = rel_bias[_rel_bucket_np(offs * r), g * N_HEADS:(g + 1) * N_HEADS].astype(F32)
        period = jnp.concatenate([vals.T, jnp.full((N_HEADS, KW - 2 * HALF), NEG_INF, F32)], axis=1)
        band = jnp.tile(period, (1, QT))[:, :QT * KW].reshape(N_HEADS, QT, KW)
        variants = []
        for v in range(4):
            ok = np.ones((1, KW), bool)
            if v & 1:
                ok &= kk >= HALF
            if v & 2:
                ok &= kk < KW - HALF
            variants.append(jnp.where(ok[None], band, NEG_INF))
        tbl = jnp.stack(variants) * LOG2E
        tables.append(jnp.swapaxes(tbl, 2, 3).astype(BF16))
    return tables


def _split3(x):
    hi = x.astype(BF16)
    rest = x - hi.astype(F32)
    mid = rest.astype(BF16)
    lo = (rest - mid.astype(F32)).astype(BF16)
    return hi, mid, lo


def _merge_proj_body(o0_ref, o1_ref, o2_ref, l0_ref, l1_ref, l2_ref, x_ref, p1_ref, p2_ref,
                     e_ref, w_ref, g_ref, b_ref, out_ref, *, bm):
    def natural(ref, p_ref, width, exact_f32):
        v = ref[...].reshape(bm, width)
        if p_ref is None:
            return v.astype(F32)
        perm = p_ref[...]
        pieces = _split3(v) if exact_f32 else (v,)
        return sum(jnp.dot(perm, piece, preferred_element_type=F32) for piece in pieces)

    perms = (None, p1_ref, p2_ref)
    lses = [natural(ref, p, LANES, True) for ref, p in zip((l0_ref, l1_ref, l2_ref), perms)]
    top = jnp.maximum(jnp.maximum(lses[0], lses[1]), lses[2])
    es = [jnp.exp(l - top) for l in lses]
    inv = 1.0 / (es[0] + es[1] + es[2])
    o = jnp.zeros((bm, D_MODEL), F32)
    for ref, p, e in zip((o0_ref, o1_ref, o2_ref), perms, es):
        hi, mid, _ = _split3(e * inv)
        wide = jnp.dot(jnp.concatenate([hi, mid], axis=1), e_ref[...], preferred_element_type=F32)
        o = o + wide * natural(ref, p, D_MODEL, False)
    h = jnp.dot(o.astype(BF16), w_ref[...], preferred_element_type=F32)
    out_ref[...] = _layer_norm(ALPHA * x_ref[...] + h, g_ref[...], b_ref[...])


def _perm_matrix(bm, r):
    n = bm // r
    p = np.arange(bm)
    m = np.zeros((bm, bm), np.float32)
    m[p, (p % r) * n + p // r] = 1.0
    return jnp.asarray(m, dtype=BF16)


def _merge_proj(outs, lses, x, w, g, b):
    n_seq, seq_len, _ = x.shape
    bm = 256
    specs = []
    for width in (D_MODEL, LANES):
        for _, r in DIL_GROUPS:
            specs.append(pl.BlockSpec((None, r, bm // r, width), lambda b_, i: (b_, 0, i, 0)))
    row = pl.BlockSpec((None, bm, D_MODEL), lambda b_, i: (b_, i, 0))
    expand = np.zeros((2 * LANES, D_MODEL), np.float32)
    for h in range(N_HEADS):
        expand[[h, LANES + h], h * HEAD_DIM:(h + 1) * HEAD_DIM] = 1.0
    vec = _const_spec((1, D_MODEL))
    return pl.pallas_call(
        functools.partial(_merge_proj_body, bm=bm),
        grid=(n_seq, seq_len // bm),
        in_specs=specs + [row, _const_spec((bm, bm)), _const_spec((bm, bm)),
                          _const_spec((2 * LANES, D_MODEL)), _const_spec((D_MODEL, D_MODEL)), vec, vec],
        out_specs=row,
        out_shape=jax.ShapeDtypeStruct((n_seq, seq_len, D_MODEL), F32),
        compiler_params=_params("parallel", "parallel"),
        name="merge_proj_ln",
    )(*outs, *lses, x, _perm_matrix(bm, DIL_GROUPS[1][1]), _perm_matrix(bm, DIL_GROUPS[2][1]),
      jnp.asarray(expand, dtype=BF16), w, g, b)


def _ffn_rows(x, wg_ref, wu_ref, wd_ref, g, b):
    xb = x.astype(BF16)
    acc = jnp.zeros(x.shape, F32)
    for c in range(D_FF // FF_CHUNK):
        cols = slice(c * FF_CHUNK, (c + 1) * FF_CHUNK)
        gate = jnp.dot(xb, wg_ref[:, cols], preferred_element_type=F32)
        up = jnp.dot(xb, wu_ref[:, cols], preferred_element_type=F32)
        h = (gate * jax.nn.sigmoid(gate) * up).astype(BF16)
        acc = acc + jnp.dot(h, wd_ref[cols, :], preferred_element_type=F32)
    return _layer_norm(ALPHA * x + acc, g, b)


def _ffn_body(x_ref, wg_ref, wu_ref, wd_ref, g_ref, b_ref, out_ref):
    out_ref[...] = _ffn_rows(x_ref[...], wg_ref, wu_ref, wd_ref, g_ref[...], b_ref[...])


def _ffn_digit_major_body(x_ref, wg_ref, wu_ref, wd_ref, g_ref, b_ref, out_ref, x2, *, n1):
    _stage_lane_tiles(x2, x_ref[...].reshape(n1 * DFT_SLAB, D_MODEL))
    per_pass = FF_ROWS // n1
    for h in range(DFT_SLAB // per_pass):
        x = jnp.concatenate([_strided_rows(x2, t, n1, DFT_SLAB)
                             for t in range(h * per_pass, (h + 1) * per_pass)], axis=0)
        out_ref[h * FF_ROWS:(h + 1) * FF_ROWS, :] = _ffn_rows(x, wg_ref, wu_ref, wd_ref,
                                                               g_ref[...], b_ref[...])


def _ffn(x, wg, wu, wd, g, b, digit_major):
    vec = _const_spec((1, D_MODEL))
    weights = [_const_spec((D_MODEL, D_FF)), _const_spec((D_MODEL, D_FF)), _const_spec((D_FF, D_MODEL))]
    if digit_major:
        n_seq, n1, n2, _ = x.shape
        bm = n1 * DFT_SLAB
        body = functools.partial(_ffn_digit_major_body, n1=n1)
        x_spec = pl.BlockSpec((None, n1, DFT_SLAB, D_MODEL), lambda b_, i: (b_, 0, i, 0))
        seq_len = n1 * n2
        scratch = [_lane_tile_scratch(bm, F32)]
    else:
        n_seq, seq_len, _ = x.shape
        bm = FF_ROWS
        body = _ffn_body
        x_spec = pl.BlockSpec((None, bm, D_MODEL), lambda b_, i: (b_, i, 0))
        scratch = []
    return pl.pallas_call(
        body,
        grid=(n_seq, seq_len // bm),
        in_specs=[x_spec] + weights + [vec, vec],
        out_specs=pl.BlockSpec((None, bm, D_MODEL), lambda b_, i: (b_, i, 0)),
        out_shape=jax.ShapeDtypeStruct((n_seq, seq_len, D_MODEL), F32),
        scratch_shapes=scratch,
        compiler_params=_params("parallel", "parallel"),
        name="swiglu_ln_digit_major" if digit_major else "swiglu_ln",
    )(x, wg, wu, wd, g, b)


def _dft_factors(seq_len):
    bits = seq_len.bit_length() - 1
    assert 1 << bits == seq_len
    n1 = 1 << ((bits + 1) // 2)
    return n1, seq_len // n1


def _dft_tables(seq_len):
    n1, n2 = _dft_factors(seq_len)
    c = np.arange(FGROUP)
    ang = 2 * np.pi * ((c[:, None] * c[None, :]) % FGROUP) / FGROUP
    wc = np.concatenate([np.cos(ang), np.sin(ang)], axis=1) / math.sqrt(FGROUP)
    k1 = np.arange(n1)
    ang = 2 * np.pi * ((k1[:, None] * k1[None, :]) % n1) / n1
    w1 = np.block([[np.cos(ang), -np.sin(ang)], [np.sin(ang), np.cos(ang)]])
    k = k1[:, None, None] + n1 * np.arange(n2)[None, :, None]
    ang = 2 * np.pi * ((k * np.arange(n2)[None, None, :]) % seq_len) / seq_len
    m2c = np.cos(ang) / math.sqrt(seq_len)
    m2s = -np.sin(ang) / math.sqrt(seq_len)
    return tuple(jnp.asarray(t, dtype=BF16) for t in (wc, w1, m2c, m2s))


def _bf16_bits(x):
    u = lax.bitcast_convert_type(x, U32)
    return u + jnp.uint32(0x7FFF) + ((u >> 16) & jnp.uint32(1))


def _pack_complex(re, im):
    return (_bf16_bits(re) >> 16) | (_bf16_bits(im) & jnp.uint32(0xFFFF0000))


def _unpack_complex(word):
    re = lax.bitcast_convert_type(word << 16, F32)
    im = lax.bitcast_convert_type(word & jnp.uint32(0xFFFF0000), F32)
    return re.astype(BF16), im.astype(BF16)


def _dft_front_body(x_ref, wc_ref, w1_ref, a_ref, x2, xb, zbuf, *, n1):
    _stage_lane_tiles(x2, x_ref[...].reshape(n1 * DFT_SLAB, D_MODEL))
    for t in range(DFT_SLAB):
        xb[t * n1:(t + 1) * n1, :] = _strided_rows(x2, t, n1, DFT_SLAB).astype(BF16)
    for g in range(N_FGROUPS):
        cols = slice(g * FGROUP, (g + 1) * FGROUP)
        y = jnp.dot(xb[:, cols], wc_ref[...], preferred_element_type=F32)
        for t in range(DFT_SLAB):
            rows = slice(t * n1, (t + 1) * n1)
            zbuf[t, 0:n1, cols] = y[rows, :FGROUP].astype(BF16)
            zbuf[t, n1:, cols] = y[rows, FGROUP:].astype(BF16)
    for t in range(DFT_SLAB):
        a = jnp.dot(w1_ref[...], zbuf[t], preferred_element_type=F32)
        a_ref[t] = _pack_complex(a[:n1], a[n1:])


def _dft_back_body(a_ref, x_ref, mc_ref, ms_ref, w_ref, bo_ref, g_ref, b_ref, out_ref, a2, x2, fbuf,
                   *, n2, kb):
    _stage_lane_tiles(a2, a_ref[...].reshape(n2 * kb, D_MODEL))
    _stage_lane_tiles(x2, x_ref[...].reshape(n2 * kb, D_MODEL))
    for q in range(kb):
        re, im = _unpack_complex(_strided_rows(a2, q, n2, kb))
        f = (jnp.dot(mc_ref[q], re, preferred_element_type=F32)
             + jnp.dot(ms_ref[q], im, preferred_element_type=F32))
        fbuf[q * n2:(q + 1) * n2, :] = f.astype(BF16)
    h = jnp.dot(fbuf[...], w_ref[...], preferred_element_type=F32) + bo_ref[...]
    for q in range(kb):
        y = ALPHA * _strided_rows(x2, q, n2, kb) + h[q * n2:(q + 1) * n2]
        out_ref[q] = _layer_norm(y, g_ref[...], b_ref[...])


def _fourier_mixer(x, w, bo, g, b):
    n_seq, seq_len, _ = x.shape
    n1, n2 = _dft_factors(seq_len)
    wc, w1, m2c, m2s = _dft_tables(seq_len)
    sb = DFT_SLAB
    a = pl.pallas_call(
        functools.partial(_dft_front_body, n1=n1),
        grid=(n_seq, n2 // sb),
        in_specs=[pl.BlockSpec((None, n1, sb, D_MODEL), lambda b_, j: (b_, 0, j, 0)),
                  _const_spec((FGROUP, 2 * FGROUP)), _const_spec((2 * n1, 2 * n1))],
        out_specs=pl.BlockSpec((None, sb, n1, D_MODEL), lambda b_, j: (b_, j, 0, 0)),
        out_shape=jax.ShapeDtypeStruct((n_seq, n2, n1, D_MODEL), U32),
        scratch_shapes=[_lane_tile_scratch(sb * n1, F32),
                        pltpu.VMEM((sb * n1, D_MODEL), BF16),
                        pltpu.VMEM((sb, 2 * n1, D_MODEL), BF16)],
        compiler_params=_params("parallel", "parallel"),
        name="dft_front",
    )(x.reshape(n_seq, n1, n2, D_MODEL), wc, w1)
    kb = DFT_SLAB
    slab = pl.BlockSpec((None, n2, kb, D_MODEL), lambda b_, k: (b_, 0, k, 0))
    m_spec = pl.BlockSpec((kb, n2, n2), lambda b_, k: (k, 0, 0))
    vec = _const_spec((1, D_MODEL))
    return pl.pallas_call(
        functools.partial(_dft_back_body, n2=n2, kb=kb),
        grid=(n_seq, n1 // kb),
        in_specs=[slab, slab, m_spec, m_spec, _const_spec((D_MODEL, D_MODEL)), vec, vec, vec],
        out_specs=pl.BlockSpec((None, kb, n2, D_MODEL), lambda b_, k: (b_, k, 0, 0)),
        out_shape=jax.ShapeDtypeStruct((n_seq, n1, n2, D_MODEL), F32),
        scratch_shapes=[_lane_tile_scratch(kb * n2, U32), _lane_tile_scratch(kb * n2, F32),
                        pltpu.VMEM((kb * n2, D_MODEL), BF16)],
        compiler_params=_params("parallel", "parallel"),
        name="dft_back_proj_ln",
    )(a, x.reshape(n_seq, n2, n1, D_MODEL), m2c, m2s, w, bo, g, b)


def _trunk(x, p):
    for i in range(DEPTH):
        li = i // 2
        ln1 = (p["ln1_g"][i], p["ln1_b"][i])
        mixer_a = i % 2 == 0
        if mixer_a:
            outs, lses = [], []
            for g, (_, r) in enumerate(DIL_GROUPS):
                o, lse = _attention_group(_qkv_group(x, p["w_qkv"][li][g], r), p["band_bias"][g])
                outs.append(o)
                lses.append(lse)
            x = _merge_proj(outs, lses, x, p["w_o_a"][li], *ln1)
        else:
            x = _fourier_mixer(x, p["w_o_b"][li], p["b_o_b"][li], *ln1)
        x = _ffn(x, p["w_gate"][i], p["w_up"][i], p["w_down"][i], p["ln2_g"][i], p["ln2_b"][i],
                 digit_major=not mixer_a)
    return x


def kernel(x_prompt, x_sample, rel_bias, w_qkv_a, w_o_a, w_o_b, b_o_b, w_gate, w_up, w_down,
           ln1_g, ln1_b, ln2_g, ln2_b):
    n_a = w_qkv_a.shape[0]
    w5 = w_qkv_a.reshape(n_a, D_MODEL, 3, N_GROUPS, D_MODEL)
    w5 = w5 * jnp.asarray([LOG2E * HEAD_DIM ** -0.5, 1.0, 1.0], F32)[None, None, :, None, None]
    w_qkv = jnp.transpose(w5, (0, 3, 1, 2, 4)).reshape(n_a, N_GROUPS, D_MODEL, 3 * D_MODEL).astype(BF16)
    vec = lambda a: a.astype(F32)[:, None, :]
    p = {
        "w_qkv": w_qkv,
        "w_o_a": w_o_a.astype(BF16),
        "w_o_b": w_o_b.astype(BF16),
        "b_o_b": vec(b_o_b),
        "w_gate": w_gate.astype(BF16),
        "w_up": w_up.astype(BF16),
        "w_down": w_down.astype(BF16),
        "ln1_g": vec(ln1_g), "ln1_b": vec(ln1_b), "ln2_g": vec(ln2_g), "ln2_b": vec(ln2_b),
        "band_bias": _band_bias(rel_bias),
    }
    return _trunk(x_prompt, p), _trunk(x_sample, p)
```

```python
import functools
import math

import numpy as np
import jax
import jax.numpy as jnp
from jax import lax
from jax.experimental import pallas as pl
from jax.experimental.pallas import tpu as pltpu

D_MODEL = 1024
DEPTH = 4
HEAD_DIM = 64
N_HEADS = 16
DIL_GROUPS = ((128, 1), (512, 4), (2048, 16))
N_GROUPS = len(DIL_GROUPS)
N_BUCKETS = 32
MAX_DISTANCE = 1024
N_FGROUPS = 4
FGROUP = D_MODEL // N_FGROUPS
D_FF = 2816
ALPHA = (2 * DEPTH) ** 0.25
LN_EPS = 1e-5
NEG_INF = -1e30
LOG2E = math.log2(math.e)
LN2 = math.log(2.0)

HALF = 64
QT = 128
KW = QT + 2 * HALF
LANES = 128
MXU_DIM = 256
FF_CHUNK = MXU_DIM
FF_ROWS = 512
DFT_SLAB = 8
VMEM_LIMIT = 52 * 2 ** 20

BF16 = jnp.bfloat16
F32 = jnp.float32
U32 = jnp.uint32


def _params(*sem):
    return pltpu.CompilerParams(dimension_semantics=sem, vmem_limit_bytes=VMEM_LIMIT)


def _const_spec(shape):
    nd = len(shape)
    return pl.BlockSpec(shape, lambda *_: (0,) * nd, pipeline_mode=pl.Buffered(1))


def _layer_norm(y, g, b):
    mu = jnp.mean(y, axis=-1, keepdims=True)
    yc = y - mu
    var = jnp.mean(yc * yc, axis=-1, keepdims=True)
    return yc * lax.rsqrt(var + LN_EPS) * g + b


def _lane_tile_scratch(rows, dtype):
    return pltpu.VMEM((D_MODEL // LANES, rows, LANES), dtype)


def _stage_lane_tiles(stage, v):
    for j in range(D_MODEL // LANES):
        stage[j] = v[:, j * LANES:(j + 1) * LANES]


def _strided_rows(stage, start, size, stride):
    return jnp.concatenate([stage[j, pl.ds(start, size, stride=stride), :]
                            for j in range(D_MODEL // LANES)], axis=1)


def _qkv_body(x_ref, w_ref, o_ref, xs, *stage, r, n):
    if r == 1:
        xs[...] = x_ref[...].astype(BF16)
    else:
        _stage_lane_tiles(stage[0], x_ref[...])
        for c in range(r):
            xs[c * n:(c + 1) * n, :] = _strided_rows(stage[0], c, n, r).astype(BF16)
    for j in range(3):
        cols = slice(j * D_MODEL, (j + 1) * D_MODEL)
        res = jnp.dot(xs[...], w_ref[:, cols], preferred_element_type=F32)
        o_ref[:, :, cols] = res.reshape(r, n, D_MODEL).astype(BF16)


def _qkv_group(x, w, r):
    n_seq, seq_len, _ = x.shape
    bm = 512
    n = bm // r
    return pl.pallas_call(
        functools.partial(_qkv_body, r=r, n=n),
        grid=(n_seq, seq_len // bm),
        in_specs=[pl.BlockSpec((None, bm, D_MODEL), lambda b, i: (b, i, 0)),
                  _const_spec((D_MODEL, 3 * D_MODEL))],
        out_specs=pl.BlockSpec((None, r, n, 3 * D_MODEL), lambda b, i: (b, 0, i, 0)),
        out_shape=jax.ShapeDtypeStruct((n_seq, r, seq_len // r, 3 * D_MODEL), BF16),
        scratch_shapes=[pltpu.VMEM((bm, D_MODEL), BF16)] + ([] if r == 1 else [_lane_tile_scratch(bm, F32)]),
        compiler_params=_params("parallel", "parallel"),
        name=f"qkv_r{r}",
    )(x, w)


def _attn_body(q_ref, kp_ref, kc_ref, kn_ref, vp_ref, vc_ref, vn_ref, bias_ref,
               o_ref, lse_ref, kbuf, vbuf, *, tile, n_tiles):
    i = pl.program_id(2)
    kbuf[0:HALF, :] = kp_ref[...]
    kbuf[HALF:HALF + tile, :] = kc_ref[...]
    kbuf[HALF + tile:, :] = kn_ref[...]
    for hp in range(N_HEADS // 2):
        src = slice(hp * LANES, (hp + 1) * LANES)
        dst = slice(2 * hp * LANES, (2 * hp + 1) * LANES)
        vbuf[0:HALF, dst] = vp_ref[:, src]
        vbuf[HALF:HALF + tile, dst] = vc_ref[:, src]
        vbuf[HALF + tile:, dst] = vn_ref[:, src]

    @pl.when((pl.program_id(0) == 0) & (pl.program_id(1) == 0) & (i == 0))
    def _():
        for hp in range(N_HEADS // 2):
            vbuf[:, (2 * hp + 1) * LANES:(2 * hp + 2) * LANES] = jnp.ones((tile + 2 * HALF, LANES), BF16)

    lane = lax.broadcasted_iota(jnp.int32, (QT, LANES), 1)
    first_head = lane < HEAD_DIM
    n_sub = tile // QT

    def sub_tile(t, carry):
        r0 = pl.multiple_of(t * QT, QT)
        at_start = jnp.logical_and(i == 0, t == 0).astype(jnp.int32)
        at_end = jnp.logical_and(i == n_tiles - 1, t == n_sub - 1).astype(jnp.int32)
        variant = at_start + 2 * at_end
        lse_tile = jnp.zeros((QT, LANES), F32)
        for hp in range(N_HEADS // 2):
            cols = slice(hp * LANES, (hp + 1) * LANES)
            q2 = q_ref[pl.ds(r0, QT), cols]
            kw = kbuf[pl.ds(r0, KW), cols]
            v1 = vbuf[pl.ds(r0, KW), 2 * hp * LANES:(2 * hp + 2) * LANES]
            zero = jnp.zeros_like(q2)
            q_pair = jnp.concatenate([jnp.where(first_head, q2, zero), jnp.where(first_head, zero, q2)], axis=0)
            s = lax.dot_general(q_pair, kw, (((1,), (1,)), ((), ())),
                                preferred_element_type=F32) + bias_ref[variant, hp]
            m = jnp.max(s, axis=-1, keepdims=True)
            p = jnp.exp2(s - m).astype(BF16)
            pv = jnp.dot(p, v1, preferred_element_type=F32)
            out = jnp.where(first_head, pv[:QT, :LANES], pv[QT:, :LANES])
            den = jnp.where(first_head, pv[:QT, LANES:], pv[QT:, LANES:])
            o_ref[pl.ds(r0, QT), cols] = (out * (1.0 / den)).astype(BF16)
            lse = jnp.where(first_head, m[:QT], m[QT:]) * LN2 + jnp.log(den)
            lse_tile = jnp.where((lane & (HEAD_DIM - 1)) == hp, lse, lse_tile)
        lse_ref[pl.ds(r0, QT), :] = lse_tile
        return carry

    lax.fori_loop(0, n_sub, sub_tile, 0)


def _attention_group(qkv, bias_t):
    n_seq, r, sub_len, _ = qkv.shape
    tile = min(512, sub_len)
    assert sub_len % tile == 0 and tile % QT == 0
    n_tiles = sub_len // tile
    hb = tile // HALF
    last_hb = sub_len // HALF - 1

    def cur(which):
        return pl.BlockSpec((None, None, tile, D_MODEL), lambda b, c, i: (b, c, i, which))

    def prev(which):
        return pl.BlockSpec((None, None, HALF, D_MODEL),
                            lambda b, c, i: (b, c, jnp.maximum(i * hb - 1, 0), which))

    def nxt(which):
        return pl.BlockSpec((None, None, HALF, D_MODEL),
                            lambda b, c, i: (b, c, jnp.minimum((i + 1) * hb, last_hb), which))

    return pl.pallas_call(
        functools.partial(_attn_body, tile=tile, n_tiles=n_tiles),
        grid=(n_seq, r, n_tiles),
        in_specs=[cur(0), prev(1), cur(1), nxt(1), prev(2), cur(2), nxt(2),
                  _const_spec((4, N_HEADS // 2, 2 * QT, KW))],
        out_specs=[pl.BlockSpec((None, None, tile, D_MODEL), lambda b, c, i: (b, c, i, 0)),
                   pl.BlockSpec((None, None, tile, LANES), lambda b, c, i: (b, c, i, 0))],
        out_shape=[jax.ShapeDtypeStruct((n_seq, r, sub_len, D_MODEL), BF16),
                   jax.ShapeDtypeStruct((n_seq, r, sub_len, LANES), F32)],
        scratch_shapes=[pltpu.VMEM((tile + 2 * HALF, D_MODEL), BF16),
                        pltpu.VMEM((tile + 2 * HALF, 2 * D_MODEL), BF16)],
        compiler_params=_params("arbitrary", "arbitrary", "arbitrary"),
        name=f"attn_r{r}",
    )(qkv, qkv, qkv, qkv, qkv, qkv, qkv, bias_t)


def _rel_bucket_np(rel):
    nb = N_BUCKETS // 2
    max_exact = nb // 2
    ret = np.where(rel > 0, nb, 0)
    n = np.abs(rel)
    nf = np.maximum(n, 1).astype(np.float32)
    large = max_exact + (np.log(nf / np.float32(max_exact)) / np.float32(math.log(MAX_DISTANCE / max_exact))
                         * np.float32(nb - max_exact)).astype(np.int32)
    large = np.minimum(large, nb - 1)
    return ret + np.where(n < max_exact, n, large)


def _band_bias(rel_bias):
    kk = np.arange(KW)[None, :]
    tables = []
    for g, (_, r) in enumerate(DIL_GROUPS):
        onehot = np.zeros((N_BUCKETS, 2 * HALF + 1), np.float32)
        onehot[_rel_bucket_np(np.arange(-HALF, HALF + 1) * r), np.arange(2 * HALF + 1)] = 1.0
        vals = jnp.dot(rel_bias[:, g * N_HEADS:(g + 1) * N_HEADS].astype(F32).T, onehot,
                       precision=lax.Precision.HIGHEST)
        period = jnp.concatenate([vals, jnp.full((N_HEADS, KW - 2 * HALF), NEG_INF, F32)], axis=1)
        band = jnp.broadcast_to(period[:, None, :], (N_HEADS, QT, KW + 1)).reshape(N_HEADS, QT * (KW + 1))
        band = band[:, :QT * KW].reshape(N_HEADS, QT, KW)
        variants = []
        for v in range(4):
            ok = np.ones((1, KW), bool)
            if v & 1:
                ok &= kk >= HALF
            if v & 2:
                ok &= kk < KW - HALF
            variants.append(jnp.where(ok[None], band, NEG_INF))
        tables.append((jnp.stack(variants) * LOG2E).reshape(4, N_HEADS // 2, 2 * QT, KW))
    return tables


def _split3(x):
    hi = x.astype(BF16)
    rest = x - hi.astype(F32)
    mid = rest.astype(BF16)
    lo = (rest - mid.astype(F32)).astype(BF16)
    return hi, mid, lo


def _merge_proj_body(o0_ref, o1_ref, o2_ref, l0_ref, l1_ref, l2_ref, x_ref, p1_ref, p2_ref,
                     e_ref, w_ref, g_ref, b_ref, out_ref, *, bm):
    def natural(ref, p_ref, width, exact_f32):
        v = ref[...].reshape(bm, width)
        if p_ref is None:
            return v.astype(F32)
        perm = p_ref[...]
        pieces = _split3(v) if exact_f32 else (v,)
        return sum(jnp.dot(perm, piece, preferred_element_type=F32) for piece in pieces)

    perms = (None, p1_ref, p2_ref)
    lses = [natural(ref, p, LANES, True) for ref, p in zip((l0_ref, l1_ref, l2_ref), perms)]
    top = jnp.maximum(jnp.maximum(lses[0], lses[1]), lses[2])
    es = [jnp.exp(l - top) for l in lses]
    inv = 1.0 / (es[0] + es[1] + es[2])
    o = jnp.zeros((bm, D_MODEL), F32)
    for ref, p, e in zip((o0_ref, o1_ref, o2_ref), perms, es):
        hi, mid, _ = _split3(e * inv)
        wide = jnp.dot(jnp.concatenate([hi, mid], axis=1), e_ref[...], preferred_element_type=F32)
        o = o + wide * natural(ref, p, D_MODEL, False)
    h = jnp.dot(o.astype(BF16), w_ref[...], preferred_element_type=F32)
    out_ref[...] = _layer_norm(ALPHA * x_ref[...] + h, g_ref[...], b_ref[...])


def _perm_matrix(bm, r):
    n = bm // r
    p = np.arange(bm)
    m = np.zeros((bm, bm), np.float32)
    m[p, (p % r) * n + p // r] = 1.0
    return jnp.asarray(m, dtype=BF16)


def _merge_proj(outs, lses, x, w, g, b):
    n_seq, seq_len, _ = x.shape
    bm = 256
    specs = []
    for width in (D_MODEL, LANES):
        for _, r in DIL_GROUPS:
            specs.append(pl.BlockSpec((None, r, bm // r, width), lambda b_, i: (b_, 0, i, 0)))
    row = pl.BlockSpec((None, bm, D_MODEL), lambda b_, i: (b_, i, 0))
    expand = np.zeros((2 * LANES, D_MODEL), np.float32)
    for h in range(N_HEADS):
        lane = HEAD_DIM * (h % 2) + h // 2
        expand[[lane, LANES + lane], h * HEAD_DIM:(h + 1) * HEAD_DIM] = 1.0
    vec = _const_spec((1, D_MODEL))
    return pl.pallas_call(
        functools.partial(_merge_proj_body, bm=bm),
        grid=(n_seq, seq_len // bm),
        in_specs=specs + [row, _const_spec((bm, bm)), _const_spec((bm, bm)),
                          _const_spec((2 * LANES, D_MODEL)), _const_spec((D_MODEL, D_MODEL)), vec, vec],
        out_specs=row,
        out_shape=jax.ShapeDtypeStruct((n_seq, seq_len, D_MODEL), F32),
        compiler_params=_params("parallel", "parallel"),
        name="merge_proj_ln",
    )(*outs, *lses, x, _perm_matrix(bm, DIL_GROUPS[1][1]), _perm_matrix(bm, DIL_GROUPS[2][1]),
      jnp.asarray(expand, dtype=BF16), w, g, b)


def _ffn_rows(x, wg_ref, wu_ref, wd_ref, g, b):
    xb = x.astype(BF16)
    acc = jnp.zeros(x.shape, F32)
    for c in range(D_FF // FF_CHUNK):
        cols = slice(c * FF_CHUNK, (c + 1) * FF_CHUNK)
        gate = jnp.dot(xb, wg_ref[:, cols], preferred_element_type=F32)
        up = jnp.dot(xb, wu_ref[:, cols], preferred_element_type=F32)
        h = (gate * jax.nn.sigmoid(gate) * up).astype(BF16)
        acc = acc + jnp.dot(h, wd_ref[cols, :], preferred_element_type=F32)
    return _layer_norm(ALPHA * x + acc, g, b)


def _ffn_body(x_ref, wg_ref, wu_ref, wd_ref, g_ref, b_ref, out_ref):
    out_ref[...] = _ffn_rows(x_ref[...], wg_ref, wu_ref, wd_ref, g_ref[...], b_ref[...])


def _ffn_digit_major_body(x_ref, wg_ref, wu_ref, wd_ref, g_ref, b_ref, out_ref, x2, *, n1):
    _stage_lane_tiles(x2, x_ref[...].reshape(n1 * DFT_SLAB, D_MODEL))
    per_pass = FF_ROWS // n1
    for h in range(DFT_SLAB // per_pass):
        x = jnp.concatenate([_strided_rows(x2, t, n1, DFT_SLAB)
                             for t in range(h * per_pass, (h + 1) * per_pass)], axis=0)
        out_ref[h * FF_ROWS:(h + 1) * FF_ROWS, :] = _ffn_rows(x, wg_ref, wu_ref, wd_ref,
                                                               g_ref[...], b_ref[...])


def _ffn(x, wg, wu, wd, g, b, digit_major):
    vec = _const_spec((1, D_MODEL))
    weights = [_const_spec((D_MODEL, D_FF)), _const_spec((D_MODEL, D_FF)), _const_spec((D_FF, D_MODEL))]
    if digit_major:
        n_seq, n1, n2, _ = x.shape
        bm = n1 * DFT_SLAB
        body = functools.partial(_ffn_digit_major_body, n1=n1)
        x_spec = pl.BlockSpec((None, n1, DFT_SLAB, D_MODEL), lambda b_, i: (b_, 0, i, 0))
        seq_len = n1 * n2
        scratch = [_lane_tile_scratch(bm, F32)]
    else:
        n_seq, seq_len, _ = x.shape
        bm = FF_ROWS
        body = _ffn_body
        x_spec = pl.BlockSpec((None, bm, D_MODEL), lambda b_, i: (b_, i, 0))
        scratch = []
    return pl.pallas_call(
        body,
        grid=(n_seq, seq_len // bm),
        in_specs=[x_spec] + weights + [vec, vec],
        out_specs=pl.BlockSpec((None, bm, D_MODEL), lambda b_, i: (b_, i, 0)),
        out_shape=jax.ShapeDtypeStruct((n_seq, seq_len, D_MODEL), F32),
        scratch_shapes=scratch,
        compiler_params=_params("parallel", "parallel"),
        name="swiglu_ln_digit_major" if digit_major else "swiglu_ln",
    )(x, wg, wu, wd, g, b)


def _dft_factors(seq_len):
    bits = seq_len.bit_length() - 1
    assert 1 << bits == seq_len
    n1 = 1 << ((bits + 1) // 2)
    return n1, seq_len // n1


def _dft_tables(seq_len):
    n1, n2 = _dft_factors(seq_len)
    c = np.arange(FGROUP)
    ang = 2 * np.pi * ((c[:, None] * c[None, :]) % FGROUP) / FGROUP
    wc = np.concatenate([np.cos(ang), np.sin(ang)], axis=1) / math.sqrt(FGROUP)
    k1 = np.arange(n1)
    ang = 2 * np.pi * ((k1[:, None] * k1[None, :]) % n1) / n1
    w1 = np.block([[np.cos(ang), -np.sin(ang)], [np.sin(ang), np.cos(ang)]])
    k = k1[:, None, None] + n1 * np.arange(n2)[None, :, None]
    ang = 2 * np.pi * ((k * np.arange(n2)[None, None, :]) % seq_len) / seq_len
    m2c = np.cos(ang) / math.sqrt(seq_len)
    m2s = -np.sin(ang) / math.sqrt(seq_len)
    return tuple(jnp.asarray(t, dtype=BF16) for t in (wc, w1, m2c, m2s))


def _bf16_bits(x):
    u = lax.bitcast_convert_type(x, U32)
    return u + jnp.uint32(0x7FFF) + ((u >> 16) & jnp.uint32(1))


def _pack_complex(re, im):
    return (_bf16_bits(re) >> 16) | (_bf16_bits(im) & jnp.uint32(0xFFFF0000))


def _unpack_complex(word):
    re = lax.bitcast_convert_type(word << 16, F32)
    im = lax.bitcast_convert_type(word & jnp.uint32(0xFFFF0000), F32)
    return re.astype(BF16), im.astype(BF16)


def _dft_front_body(x_ref, wc_ref, w1_ref, a_ref, x2, xb, zbuf, *, n1):
    _stage_lane_tiles(x2, x_ref[...].reshape(n1 * DFT_SLAB, D_MODEL))
    for t in range(DFT_SLAB):
        xb[t * n1:(t + 1) * n1, :] = _strided_rows(x2, t, n1, DFT_SLAB).astype(BF16)
    for g in range(N_FGROUPS):
        cols = slice(g * FGROUP, (g + 1) * FGROUP)
        y = jnp.dot(xb[:, cols], wc_ref[...], preferred_element_type=F32)
        for t in range(DFT_SLAB):
            rows = slice(t * n1, (t + 1) * n1)
            zbuf[t, 0:n1, cols] = y[rows, :FGROUP].astype(BF16)
            zbuf[t, n1:, cols] = y[rows, FGROUP:].astype(BF16)
    for t in range(DFT_SLAB):
        a = jnp.dot(w1_ref[...], zbuf[t], preferred_element_type=F32)
        a_ref[t] = _pack_complex(a[:n1], a[n1:])


def _dft_back_body(a_ref, x_ref, mc_ref, ms_ref, w_ref, bo_ref, g_ref, b_ref, out_ref, a2, x2, fbuf,
                   *, n2, kb):
    _stage_lane_tiles(a2, a_ref[...].reshape(n2 * kb, D_MODEL))
    _stage_lane_tiles(x2, x_ref[...].reshape(n2 * kb, D_MODEL))
    for q in range(kb):
        re, im = _unpack_complex(_strided_rows(a2, q, n2, kb))
        f = (jnp.dot(mc_ref[q], re, preferred_element_type=F32)
             + jnp.dot(ms_ref[q], im, preferred_element_type=F32))
        fbuf[q * n2:(q + 1) * n2, :] = f.astype(BF16)
    h = jnp.dot(fbuf[...], w_ref[...], preferred_element_type=F32) + bo_ref[...]
    for q in range(kb):
        y = ALPHA * _strided_rows(x2, q, n2, kb) + h[q * n2:(q + 1) * n2]
        out_ref[q] = _layer_norm(y, g_ref[...], b_ref[...])


def _fourier_mixer(x, w, bo, g, b):
    n_seq, seq_len, _ = x.shape
    n1, n2 = _dft_factors(seq_len)
    wc, w1, m2c, m2s = _dft_tables(seq_len)
    sb = DFT_SLAB
    a = pl.pallas_call(
        functools.partial(_dft_front_body, n1=n1),
        grid=(n_seq, n2 // sb),
        in_specs=[pl.BlockSpec((None, n1, sb, D_MODEL), lambda b_, j: (b_, 0, j, 0)),
                  _const_spec((FGROUP, 2 * FGROUP)), _const_spec((2 * n1, 2 * n1))],
        out_specs=pl.BlockSpec((None, sb, n1, D_MODEL), lambda b_, j: (b_, j, 0, 0)),
        out_shape=jax.ShapeDtypeStruct((n_seq, n2, n1, D_MODEL), U32),
        scratch_shapes=[_lane_tile_scratch(sb * n1, F32),
                        pltpu.VMEM((sb * n1, D_MODEL), BF16),
                        pltpu.VMEM((sb, 2 * n1, D_MODEL), BF16)],
        compiler_params=_params("parallel", "parallel"),
        name="dft_front",
    )(x.reshape(n_seq, n1, n2, D_MODEL), wc, w1)
    kb = DFT_SLAB
    slab = pl.BlockSpec((None, n2, kb, D_MODEL), lambda b_, k: (b_, 0, k, 0))
    m_spec = pl.BlockSpec((kb, n2, n2), lambda b_, k: (k, 0, 0))
    vec = _const_spec((1, D_MODEL))
    return pl.pallas_call(
        functools.partial(_dft_back_body, n2=n2, kb=kb),
        grid=(n_seq, n1 // kb),
        in_specs=[slab, slab, m_spec, m_spec, _const_spec((D_MODEL, D_MODEL)), vec, vec, vec],
        out_specs=pl.BlockSpec((None, kb, n2, D_MODEL), lambda b_, k: (b_, k, 0, 0)),
        out_shape=jax.ShapeDtypeStruct((n_seq, n1, n2, D_MODEL), F32),
        scratch_shapes=[_lane_tile_scratch(kb * n2, U32), _lane_tile_scratch(kb * n2, F32),
                        pltpu.VMEM((kb * n2, D_MODEL), BF16)],
        compiler_params=_params("parallel", "parallel"),
        name="dft_back_proj_ln",
    )(a, x.reshape(n_seq, n2, n1, D_MODEL), m2c, m2s, w, bo, g, b)


def _trunk(x, p):
    for i in range(DEPTH):
        li = i // 2
        ln1 = (p["ln1_g"][i], p["ln1_b"][i])
        mixer_a = i % 2 == 0
        if mixer_a:
            outs, lses = [], []
            for g, (_, r) in enumerate(DIL_GROUPS):
                o, lse = _attention_group(_qkv_group(x, p["w_qkv"][li][g], r), p["band_bias"][g])
                outs.append(o)
                lses.append(lse)
            x = _merge_proj(outs, lses, x, p["w_o_a"][li], *ln1)
        else:
            x = _fourier_mixer(x, p["w_o_b"][li], p["b_o_b"][li], *ln1)
        x = _ffn(x, p["w_gate"][i], p["w_up"][i], p["w_down"][i], p["ln2_g"][i], p["ln2_b"][i],
                 digit_major=not mixer_a)
    return x


def kernel(x_prompt, x_sample, rel_bias, w_qkv_a, w_o_a, w_o_b, b_o_b, w_gate, w_up, w_down,
           ln1_g, ln1_b, ln2_g, ln2_b):
    n_a = w_qkv_a.shape[0]
    w5 = w_qkv_a.reshape(n_a, D_MODEL, 3, N_GROUPS, D_MODEL)
    w5 = w5 * jnp.asarray([LOG2E * HEAD_DIM ** -0.5, 1.0, 1.0], F32)[None, None, :, None, None]
    w_qkv = jnp.transpose(w5, (0, 3, 1, 2, 4)).reshape(n_a, N_GROUPS, D_MODEL, 3 * D_MODEL).astype(BF16)
    vec = lambda a: a.astype(F32)[:, None, :]
    p = {
        "w_qkv": w_qkv,
        "w_o_a": w_o_a.astype(BF16),
        "w_o_b": w_o_b.astype(BF16),
        "b_o_b": vec(b_o_b),
        "w_gate": w_gate.astype(BF16),
        "w_up": w_up.astype(BF16),
        "w_down": w_down.astype(BF16),
        "ln1_g": vec(ln1_g), "ln1_b": vec(ln1_b), "ln2_g": vec(ln2_g), "ln2_b": vec(ln2_b),
        "band_bias": _band_bias(rel_bias),
    }
    return _trunk(x_prompt, p), _trunk(x_sample, p)
```

```python
import functools
import math

import numpy as np
import jax
import jax.numpy as jnp
from jax import lax
from jax.experimental import pallas as pl
from jax.experimental.pallas import tpu as pltpu

D_MODEL = 1024
DEPTH = 4
HEAD_DIM = 64
N_HEADS = 16
DIL_GROUPS = ((128, 1), (512, 4), (2048, 16))
N_GROUPS = len(DIL_GROUPS)
N_BUCKETS = 32
MAX_DISTANCE = 1024
N_FGROUPS = 4
FGROUP = D_MODEL // N_FGROUPS
D_FF = 2816
ALPHA = (2 * DEPTH) ** 0.25
LN_EPS = 1e-5
NEG_INF = -1e30
LOG2E = math.log2(math.e)
LN2 = math.log(2.0)

HALF = 64
QT = 128
KW = QT + 2 * HALF
LANES = 128
MXU_DIM = 256
FF_CHUNK = MXU_DIM
FF_ROWS = 512
DFT_SLAB = 8
VMEM_LIMIT = 52 * 2 ** 20

BF16 = jnp.bfloat16
F32 = jnp.float32
U32 = jnp.uint32


def _params(*sem):
    return pltpu.CompilerParams(dimension_semantics=sem, vmem_limit_bytes=VMEM_LIMIT)


def _const_spec(shape):
    nd = len(shape)
    return pl.BlockSpec(shape, lambda *_: (0,) * nd, pipeline_mode=pl.Buffered(1))


def _layer_norm(y, g, b):
    mu = jnp.mean(y, axis=-1, keepdims=True)
    yc = y - mu
    var = jnp.mean(yc * yc, axis=-1, keepdims=True)
    return yc * lax.rsqrt(var + LN_EPS) * g + b


def _lane_tile_scratch(rows, dtype):
    return pltpu.VMEM((D_MODEL // LANES, rows, LANES), dtype)


def _stage_lane_tiles(stage, v):
    for j in range(D_MODEL // LANES):
        stage[j] = v[:, j * LANES:(j + 1) * LANES]


def _strided_rows(stage, start, size, stride):
    return jnp.concatenate([stage[j, pl.ds(start, size, stride=stride), :]
                            for j in range(D_MODEL // LANES)], axis=1)


def _qkv_body(x_ref, w_ref, o_ref, xs, *stage, r, n):
    if r == 1:
        xs[...] = x_ref[...].astype(BF16)
    else:
        _stage_lane_tiles(stage[0], x_ref[...])
        for c in range(r):
            xs[c * n:(c + 1) * n, :] = _strided_rows(stage[0], c, n, r).astype(BF16)
    for j in range(3):
        cols = slice(j * D_MODEL, (j + 1) * D_MODEL)
        res = jnp.dot(xs[...], w_ref[:, cols], preferred_element_type=F32)
        o_ref[:, :, cols] = res.reshape(r, n, D_MODEL).astype(BF16)


def _qkv_group(x, w, r):
    n_seq, seq_len, _ = x.shape
    bm = 512
    n = bm // r
    return pl.pallas_call(
        functools.partial(_qkv_body, r=r, n=n),
        grid=(n_seq, seq_len // bm),
        in_specs=[pl.BlockSpec((None, bm, D_MODEL), lambda b, i: (b, i, 0)),
                  _const_spec((D_MODEL, 3 * D_MODEL))],
        out_specs=pl.BlockSpec((None, r, n, 3 * D_MODEL), lambda b, i: (b, 0, i, 0)),
        out_shape=jax.ShapeDtypeStruct((n_seq, r, seq_len // r, 3 * D_MODEL), BF16),
        scratch_shapes=[pltpu.VMEM((bm, D_MODEL), BF16)] + ([] if r == 1 else [_lane_tile_scratch(bm, F32)]),
        compiler_params=_params("parallel", "parallel"),
        name=f"qkv_r{r}",
    )(x, w)


def _attn_body(q_ref, kp_ref, kc_ref, kn_ref, vp_ref, vc_ref, vn_ref, bias_ref,
               o_ref, lse_ref, kbuf, vbuf, *, tile, n_tiles):
    i = pl.program_id(2)
    kbuf[0:HALF, :] = kp_ref[...]
    kbuf[HALF:HALF + tile, :] = kc_ref[...]
    kbuf[HALF + tile:, :] = kn_ref[...]
    for hp in range(N_HEADS // 2):
        src = slice(hp * LANES, (hp + 1) * LANES)
        dst = slice(2 * hp * LANES, (2 * hp + 1) * LANES)
        vbuf[0:HALF, dst] = vp_ref[:, src]
        vbuf[HALF:HALF + tile, dst] = vc_ref[:, src]
        vbuf[HALF + tile:, dst] = vn_ref[:, src]

    @pl.when((pl.program_id(0) == 0) & (pl.program_id(1) == 0) & (i == 0))
    def _():
        for hp in range(N_HEADS // 2):
            vbuf[:, (2 * hp + 1) * LANES:(2 * hp + 2) * LANES] = jnp.ones((tile + 2 * HALF, LANES), BF16)

    lane = lax.broadcasted_iota(jnp.int32, (QT, LANES), 1)
    first_head = lane < HEAD_DIM
    n_sub = tile // QT

    def sub_tile(t, carry):
        r0 = pl.multiple_of(t * QT, QT)
        at_start = jnp.logical_and(i == 0, t == 0).astype(jnp.int32)
        at_end = jnp.logical_and(i == n_tiles - 1, t == n_sub - 1).astype(jnp.int32)
        variant = at_start + 2 * at_end
        lse_tile = jnp.zeros((QT, LANES), F32)
        for hp in range(N_HEADS // 2):
            cols = slice(hp * LANES, (hp + 1) * LANES)
            q2 = q_ref[pl.ds(r0, QT), cols]
            kw = kbuf[pl.ds(r0, KW), cols]
            v1 = vbuf[pl.ds(r0, KW), 2 * hp * LANES:(2 * hp + 2) * LANES]
            zero = jnp.zeros_like(q2)
            q_pair = jnp.concatenate([jnp.where(first_head, q2, zero), jnp.where(first_head, zero, q2)], axis=0)
            s = lax.dot_general(q_pair, kw, (((1,), (1,)), ((), ())),
                                preferred_element_type=F32) + bias_ref[variant, hp]
            m = jnp.max(s, axis=-1, keepdims=True)
            p = jnp.exp2(s - m).astype(BF16)
            pv = jnp.dot(p, v1, preferred_element_type=F32)
            out = jnp.where(first_head, pv[:QT, :LANES], pv[QT:, :LANES])
            den = jnp.where(first_head, pv[:QT, LANES:], pv[QT:, LANES:])
            o_ref[pl.ds(r0, QT), cols] = (out * (1.0 / den)).astype(BF16)
            lse = jnp.where(first_head, m[:QT], m[QT:]) * LN2 + jnp.log(den)
            lse_tile = jnp.where((lane & (HEAD_DIM - 1)) == hp, lse, lse_tile)
        lse_ref[pl.ds(r0, QT), :] = lse_tile
        return carry

    lax.fori_loop(0, n_sub, sub_tile, 0, unroll=True)


def _attention_group(qkv, bias_t):
    n_seq, r, sub_len, _ = qkv.shape
    tile = min(512, sub_len)
    assert sub_len % tile == 0 and tile % QT == 0
    n_tiles = sub_len // tile
    hb = tile // HALF
    last_hb = sub_len // HALF - 1

    def cur(which):
        return pl.BlockSpec((None, None, tile, D_MODEL), lambda b, c, i: (b, c, i, which))

    def prev(which):
        return pl.BlockSpec((None, None, HALF, D_MODEL),
                            lambda b, c, i: (b, c, jnp.maximum(i * hb - 1, 0), which))

    def nxt(which):
        return pl.BlockSpec((None, None, HALF, D_MODEL),
                            lambda b, c, i: (b, c, jnp.minimum((i + 1) * hb, last_hb), which))

    return pl.pallas_call(
        functools.partial(_attn_body, tile=tile, n_tiles=n_tiles),
        grid=(n_seq, r, n_tiles),
        in_specs=[cur(0), prev(1), cur(1), nxt(1), prev(2), cur(2), nxt(2),
                  _const_spec((4, N_HEADS // 2, 2 * QT, KW))],
        out_specs=[pl.BlockSpec((None, None, tile, D_MODEL), lambda b, c, i: (b, c, i, 0)),
                   pl.BlockSpec((None, None, tile, LANES), lambda b, c, i: (b, c, i, 0))],
        out_shape=[jax.ShapeDtypeStruct((n_seq, r, sub_len, D_MODEL), BF16),
                   jax.ShapeDtypeStruct((n_seq, r, sub_len, LANES), F32)],
        scratch_shapes=[pltpu.VMEM((tile + 2 * HALF, D_MODEL), BF16),
                        pltpu.VMEM((tile + 2 * HALF, 2 * D_MODEL), BF16)],
        compiler_params=_params("arbitrary", "arbitrary", "arbitrary"),
        name=f"attn_r{r}",
    )(qkv, qkv, qkv, qkv, qkv, qkv, qkv, bias_t)


def _rel_bucket_np(rel):
    nb = N_BUCKETS // 2
    max_exact = nb // 2
    ret = np.where(rel > 0, nb, 0)
    n = np.abs(rel)
    nf = np.maximum(n, 1).astype(np.float32)
    large = max_exact + (np.log(nf / np.float32(max_exact)) / np.float32(math.log(MAX_DISTANCE / max_exact))
                         * np.float32(nb - max_exact)).astype(np.int32)
    large = np.minimum(large, nb - 1)
    return ret + np.where(n < max_exact, n, large)


def _band_bias(rel_bias):
    kk = np.arange(KW)[None, :]
    tables = []
    for g, (_, r) in enumerate(DIL_GROUPS):
        onehot = np.zeros((N_BUCKETS, 2 * HALF + 1), np.float32)
        onehot[_rel_bucket_np(np.arange(-HALF, HALF + 1) * r), np.arange(2 * HALF + 1)] = 1.0
        vals = jnp.dot(rel_bias[:, g * N_HEADS:(g + 1) * N_HEADS].astype(F32).T, onehot,
                       precision=lax.Precision.HIGHEST)
        period = jnp.concatenate([vals, jnp.full((N_HEADS, KW - 2 * HALF), NEG_INF, F32)], axis=1)
        band = jnp.broadcast_to(period[:, None, :], (N_HEADS, QT, KW + 1)).reshape(N_HEADS, QT * (KW + 1))
        band = band[:, :QT * KW].reshape(N_HEADS, QT, KW)
        variants = []
        for v in range(4):
            ok = np.ones((1, KW), bool)
            if v & 1:
                ok &= kk >= HALF
            if v & 2:
                ok &= kk < KW - HALF
            variants.append(jnp.where(ok[None], band, NEG_INF))
        tables.append((jnp.stack(variants) * LOG2E).reshape(4, N_HEADS // 2, 2 * QT, KW))
    return tables


def _split3(x):
    hi = x.astype(BF16)
    rest = x - hi.astype(F32)
    mid = rest.astype(BF16)
    lo = (rest - mid.astype(F32)).astype(BF16)
    return hi, mid, lo


def _merge_proj_body(o0_ref, o1_ref, o2_ref, l0_ref, l1_ref, l2_ref, x_ref, p1_ref, p2_ref,
                     e_ref, w_ref, g_ref, b_ref, out_ref, *, bm):
    def natural(ref, p_ref, k, width, exact_f32):
        n = bm // ref.shape[0]
        v = ref[:, k * n:(k + 1) * n, :].reshape(bm, width)
        if p_ref is None:
            return v.astype(F32)
        perm = p_ref[...]
        if not exact_f32:
            return jnp.dot(perm, v, preferred_element_type=F32)
        moved = jnp.dot(perm, jnp.concatenate(_split3(v), axis=1), preferred_element_type=F32)
        return moved[:, :width] + moved[:, width:2 * width] + moved[:, 2 * width:]

    perms = (None, p1_ref, p2_ref)
    for k in range(x_ref.shape[0] // bm):
        lses = [natural(ref, p, k, LANES, True) for ref, p in zip((l0_ref, l1_ref, l2_ref), perms)]
        top = jnp.maximum(jnp.maximum(lses[0], lses[1]), lses[2])
        es = [jnp.exp(l - top) for l in lses]
        inv = 1.0 / (es[0] + es[1] + es[2])
        o = jnp.zeros((bm, D_MODEL), F32)
        for ref, p, e in zip((o0_ref, o1_ref, o2_ref), perms, es):
            hi, mid, _ = _split3(e * inv)
            wide = jnp.dot(jnp.concatenate([hi, mid], axis=1), e_ref[...], preferred_element_type=F32)
            o = o + wide * natural(ref, p, k, D_MODEL, False)
        h = jnp.dot(o.astype(BF16), w_ref[...], preferred_element_type=F32)
        rows = slice(k * bm, (k + 1) * bm)
        out_ref[rows, :] = _layer_norm(ALPHA * x_ref[rows, :] + h, g_ref[...], b_ref[...])


def _perm_matrix(bm, r):
    n = bm // r
    p = np.arange(bm)
    m = np.zeros((bm, bm), np.float32)
    m[p, (p % r) * n + p // r] = 1.0
    return jnp.asarray(m, dtype=BF16)


def _merge_proj(outs, lses, x, w, g, b):
    n_seq, seq_len, _ = x.shape
    bm = MXU_DIM
    step = 2 * bm
    specs = []
    for width in (D_MODEL, LANES):
        for _, r in DIL_GROUPS:
            specs.append(pl.BlockSpec((None, r, step // r, width), lambda b_, i: (b_, 0, i, 0)))
    row = pl.BlockSpec((None, step, D_MODEL), lambda b_, i: (b_, i, 0))
    expand = np.zeros((2 * LANES, D_MODEL), np.float32)
    for h in range(N_HEADS):
        lane = HEAD_DIM * (h % 2) + h // 2
        expand[[lane, LANES + lane], h * HEAD_DIM:(h + 1) * HEAD_DIM] = 1.0
    vec = _const_spec((1, D_MODEL))
    return pl.pallas_call(
        functools.partial(_merge_proj_body, bm=bm),
        grid=(n_seq, seq_len // step),
        in_specs=specs + [row, _const_spec((bm, bm)), _const_spec((bm, bm)),
                          _const_spec((2 * LANES, D_MODEL)), _const_spec((D_MODEL, D_MODEL)), vec, vec],
        out_specs=row,
        out_shape=jax.ShapeDtypeStruct((n_seq, seq_len, D_MODEL), F32),
        compiler_params=_params("parallel", "parallel"),
        name="merge_proj_ln",
    )(*outs, *lses, x, _perm_matrix(bm, DIL_GROUPS[1][1]), _perm_matrix(bm, DIL_GROUPS[2][1]),
      jnp.asarray(expand, dtype=BF16), w, g, b)


def _ffn_rows(x, wg_ref, wu_ref, wd_ref, g, b):
    xb = x.astype(BF16)
    acc = jnp.zeros(x.shape, F32)
    for c in range(D_FF // FF_CHUNK):
        cols = slice(c * FF_CHUNK, (c + 1) * FF_CHUNK)
        gate = jnp.dot(xb, wg_ref[:, cols], preferred_element_type=F32)
        up = jnp.dot(xb, wu_ref[:, cols], preferred_element_type=F32)
        h = (gate * jax.nn.sigmoid(gate) * up).astype(BF16)
        acc = acc + jnp.dot(h, wd_ref[cols, :], preferred_element_type=F32)
    return _layer_norm(ALPHA * x + acc, g, b)


def _ffn_body(x_ref, wg_ref, wu_ref, wd_ref, g_ref, b_ref, out_ref):
    out_ref[...] = _ffn_rows(x_ref[...], wg_ref, wu_ref, wd_ref, g_ref[...], b_ref[...])


def _ffn_digit_major_body(x_ref, wg_ref, wu_ref, wd_ref, g_ref, b_ref, out_ref, x2, *, n1):
    _stage_lane_tiles(x2, x_ref[...].reshape(n1 * DFT_SLAB, D_MODEL))
    per_pass = FF_ROWS // n1
    for h in range(DFT_SLAB // per_pass):
        x = jnp.concatenate([_strided_rows(x2, t, n1, DFT_SLAB)
                             for t in range(h * per_pass, (h + 1) * per_pass)], axis=0)
        out_ref[h * FF_ROWS:(h + 1) * FF_ROWS, :] = _ffn_rows(x, wg_ref, wu_ref, wd_ref,
                                                               g_ref[...], b_ref[...])


def _ffn(x, wg, wu, wd, g, b, digit_major):
    vec = _const_spec((1, D_MODEL))
    weights = [_const_spec((D_MODEL, D_FF)), _const_spec((D_MODEL, D_FF)), _const_spec((D_FF, D_MODEL))]
    if digit_major:
        n_seq, n1, n2, _ = x.shape
        bm = n1 * DFT_SLAB
        body = functools.partial(_ffn_digit_major_body, n1=n1)
        x_spec = pl.BlockSpec((None, n1, DFT_SLAB, D_MODEL), lambda b_, i: (b_, 0, i, 0))
        seq_len = n1 * n2
        scratch = [_lane_tile_scratch(bm, F32)]
    else:
        n_seq, seq_len, _ = x.shape
        bm = FF_ROWS
        body = _ffn_body
        x_spec = pl.BlockSpec((None, bm, D_MODEL), lambda b_, i: (b_, i, 0))
        scratch = []
    return pl.pallas_call(
        body,
        grid=(n_seq, seq_len // bm),
        in_specs=[x_spec] + weights + [vec, vec],
        out_specs=pl.BlockSpec((None, bm, D_MODEL), lambda b_, i: (b_, i, 0)),
        out_shape=jax.ShapeDtypeStruct((n_seq, seq_len, D_MODEL), F32),
        scratch_shapes=scratch,
        compiler_params=_params("parallel", "parallel"),
        name="swiglu_ln_digit_major" if digit_major else "swiglu_ln",
    )(x, wg, wu, wd, g, b)


def _dft_factors(seq_len):
    bits = seq_len.bit_length() - 1
    assert 1 << bits == seq_len
    n1 = 1 << ((bits + 1) // 2)
    return n1, seq_len // n1


def _dft_tables(seq_len):
    n1, n2 = _dft_factors(seq_len)
    c = np.arange(FGROUP)
    ang = 2 * np.pi * ((c[:, None] * c[None, :]) % FGROUP) / FGROUP
    wc = np.concatenate([np.cos(ang), np.sin(ang)], axis=1) / math.sqrt(FGROUP)
    k1 = np.arange(n1)
    ang = 2 * np.pi * ((k1[:, None] * k1[None, :]) % n1) / n1
    w1 = np.block([[np.cos(ang), -np.sin(ang)], [np.sin(ang), np.cos(ang)]])
    k = k1[:, None, None] + n1 * np.arange(n2)[None, :, None]
    ang = 2 * np.pi * ((k * np.arange(n2)[None, None, :]) % seq_len) / seq_len
    m2 = np.concatenate([np.cos(ang), -np.sin(ang)], axis=2) / math.sqrt(seq_len)
    return tuple(jnp.asarray(t, dtype=BF16) for t in (wc, w1, m2))


def _bf16_bits(x):
    u = lax.bitcast_convert_type(x, U32)
    return u + jnp.uint32(0x7FFF) + ((u >> 16) & jnp.uint32(1))


def _pack_complex(re, im):
    return (_bf16_bits(re) >> 16) | (_bf16_bits(im) & jnp.uint32(0xFFFF0000))


def _unpack_complex(word):
    re = lax.bitcast_convert_type(word << 16, F32)
    im = lax.bitcast_convert_type(word & jnp.uint32(0xFFFF0000), F32)
    return re.astype(BF16), im.astype(BF16)


def _dft_front_body(x_ref, wc_ref, w1_ref, a_ref, x2, xb, zbuf, *, n1):
    _stage_lane_tiles(x2, x_ref[...].reshape(n1 * DFT_SLAB, D_MODEL))
    for t in range(DFT_SLAB):
        xb[t * n1:(t + 1) * n1, :] = _strided_rows(x2, t, n1, DFT_SLAB).astype(BF16)
    for g in range(N_FGROUPS):
        cols = slice(g * FGROUP, (g + 1) * FGROUP)
        y = jnp.dot(xb[:, cols], wc_ref[...], preferred_element_type=F32)
        for t in range(DFT_SLAB):
            rows = slice(t * n1, (t + 1) * n1)
            zbuf[t, 0:n1, cols] = y[rows, :FGROUP].astype(BF16)
            zbuf[t, n1:, cols] = y[rows, FGROUP:].astype(BF16)
    for t in range(DFT_SLAB):
        a = jnp.dot(w1_ref[...], zbuf[t], preferred_element_type=F32)
        a_ref[t] = _pack_complex(a[:n1], a[n1:])


def _dft_back_body(a_ref, x_ref, m_ref, w_ref, bo_ref, g_ref, b_ref, out_ref, a2, x2, fbuf,
                   *, n2, kb):
    _stage_lane_tiles(a2, a_ref[...].reshape(n2 * kb, D_MODEL))
    _stage_lane_tiles(x2, x_ref[...].reshape(n2 * kb, D_MODEL))
    half = kb // 2
    for q0 in range(0, kb, half):
        for q in range(q0, q0 + half):
            re, im = _unpack_complex(_strided_rows(a2, q, n2, kb))
            f = jnp.dot(m_ref[q], jnp.concatenate([re, im], axis=0), preferred_element_type=F32)
            fbuf[q * n2:(q + 1) * n2, :] = f.astype(BF16)
        h = jnp.dot(fbuf[q0 * n2:(q0 + half) * n2, :], w_ref[...], preferred_element_type=F32) + bo_ref[...]
        for q in range(q0, q0 + half):
            y = ALPHA * _strided_rows(x2, q, n2, kb) + h[(q - q0) * n2:(q - q0 + 1) * n2]
            out_ref[q] = _layer_norm(y, g_ref[...], b_ref[...])


def _fourier_mixer(x, w, bo, g, b):
    n_seq, seq_len, _ = x.shape
    n1, n2 = _dft_factors(seq_len)
    wc, w1, m2 = _dft_tables(seq_len)
    sb = DFT_SLAB
    a = pl.pallas_call(
        functools.partial(_dft_front_body, n1=n1),
        grid=(n_seq, n2 // sb),
        in_specs=[pl.BlockSpec((None, n1, sb, D_MODEL), lambda b_, j: (b_, 0, j, 0)),
                  _const_spec((FGROUP, 2 * FGROUP)), _const_spec((2 * n1, 2 * n1))],
        out_specs=pl.BlockSpec((None, sb, n1, D_MODEL), lambda b_, j: (b_, j, 0, 0)),
        out_shape=jax.ShapeDtypeStruct((n_seq, n2, n1, D_MODEL), U32),
        scratch_shapes=[_lane_tile_scratch(sb * n1, F32),
                        pltpu.VMEM((sb * n1, D_MODEL), BF16),
                        pltpu.VMEM((sb, 2 * n1, D_MODEL), BF16)],
        compiler_params=_params("parallel", "parallel"),
        name="dft_front",
    )(x.reshape(n_seq, n1, n2, D_MODEL), wc, w1)
    kb = DFT_SLAB
    slab = pl.BlockSpec((None, n2, kb, D_MODEL), lambda b_, k: (b_, 0, k, 0))
    m_spec = pl.BlockSpec((kb, n2, 2 * n2), lambda b_, k: (k, 0, 0))
    vec = _const_spec((1, D_MODEL))
    return pl.pallas_call(
        functools.partial(_dft_back_body, n2=n2, kb=kb),
        grid=(n_seq, n1 // kb),
        in_specs=[slab, slab, m_spec, _const_spec((D_MODEL, D_MODEL)), vec, vec, vec],
        out_specs=pl.BlockSpec((None, kb, n2, D_MODEL), lambda b_, k: (b_, k, 0, 0)),
        out_shape=jax.ShapeDtypeStruct((n_seq, n1, n2, D_MODEL), F32),
        scratch_shapes=[_lane_tile_scratch(kb * n2, U32), _lane_tile_scratch(kb * n2, F32),
                        pltpu.VMEM((kb * n2, D_MODEL), BF16)],
        compiler_params=_params("parallel", "parallel"),
        name="dft_back_proj_ln",
    )(a, x.reshape(n_seq, n2, n1, D_MODEL), m2, w, bo, g, b)


def _trunk(x, p):
    for i in range(DEPTH):
        li = i // 2
        ln1 = (p["ln1_g"][i], p["ln1_b"][i])
        mixer_a = i % 2 == 0
        if mixer_a:
            outs, lses = [], []
            for g, (_, r) in enumerate(DIL_GROUPS):
                o, lse = _attention_group(_qkv_group(x, p["w_qkv"][li][g], r), p["band_bias"][g])
                outs.append(o)
                lses.append(lse)
            x = _merge_proj(outs, lses, x, p["w_o_a"][li], *ln1)
        else:
            x = _fourier_mixer(x, p["w_o_b"][li], p["b_o_b"][li], *ln1)
        x = _ffn(x, p["w_gate"][i], p["w_up"][i], p["w_down"][i], p["ln2_g"][i], p["ln2_b"][i],
                 digit_major=not mixer_a)
    return x


def kernel(x_prompt, x_sample, rel_bias, w_qkv_a, w_o_a, w_o_b, b_o_b, w_gate, w_up, w_down,
           ln1_g, ln1_b, ln2_g, ln2_b):
    n_a = w_qkv_a.shape[0]
    w5 = w_qkv_a.reshape(n_a, D_MODEL, 3, N_GROUPS, D_MODEL)
    w5 = w5 * jnp.asarray([LOG2E * HEAD_DIM ** -0.5, 1.0, 1.0], F32)[None, None, :, None, None]
    w_qkv = jnp.transpose(w5, (0, 3, 1, 2, 4)).reshape(n_a, N_GROUPS, D_MODEL, 3 * D_MODEL).astype(BF16)
    vec = lambda a: a.astype(F32)[:, None, :]
    p = {
        "w_qkv": w_qkv,
        "w_o_a": w_o_a.astype(BF16),
        "w_o_b": w_o_b.astype(BF16),
        "b_o_b": vec(b_o_b),
        "w_gate": w_gate.astype(BF16),
        "w_up": w_up.astype(BF16),
        "w_down": w_down.astype(BF16),
        "ln1_g": vec(ln1_g), "ln1_b": vec(ln1_b), "ln2_g": vec(ln2_g), "ln2_b": vec(ln2_b),
        "band_bias": _band_bias(rel_bias),
    }
    return _trunk(x_prompt, p), _trunk(x_sample, p)
```

```python
import functools
import math

import numpy as np
import jax
import jax.numpy as jnp
from jax import lax
from jax.experimental import pallas as pl
from jax.experimental.pallas import tpu as pltpu

D_MODEL = 1024
DEPTH = 4
HEAD_DIM = 64
N_HEADS = 16
DIL_GROUPS = ((128, 1), (512, 4), (2048, 16))
N_GROUPS = len(DIL_GROUPS)
N_BUCKETS = 32
MAX_DISTANCE = 1024
N_FGROUPS = 4
FGROUP = D_MODEL // N_FGROUPS
D_FF = 2816
ALPHA = (2 * DEPTH) ** 0.25
LN_EPS = 1e-5
NEG_INF = -1e30
LOG2E = math.log2(math.e)
LN2 = math.log(2.0)

HALF = 64
QT = 128
KW = QT + 2 * HALF
ATT_ROWS = 1024
LANES = 128
MXU_DIM = 256
FF_CHUNK = MXU_DIM
FF_ROWS = 512
DFT_SLAB = 8
VMEM_LIMIT = 52 * 2 ** 20

BF16 = jnp.bfloat16
F32 = jnp.float32
U32 = jnp.uint32


def _params(*sem):
    return pltpu.CompilerParams(dimension_semantics=sem, vmem_limit_bytes=VMEM_LIMIT)


def _const_spec(shape):
    nd = len(shape)
    return pl.BlockSpec(shape, lambda *_: (0,) * nd, pipeline_mode=pl.Buffered(1))


def _layer_spec(shape, layer):
    nd = len(shape)
    return pl.BlockSpec((None,) + tuple(shape), lambda *_: (layer,) + (0,) * nd,
                        pipeline_mode=pl.Buffered(1))


def _layer_norm(y, g, b):
    mu = jnp.mean(y, axis=-1, keepdims=True)
    yc = y - mu
    var = jnp.mean(yc * yc, axis=-1, keepdims=True)
    return yc * lax.rsqrt(var + LN_EPS) * g + b


def _lane_tile_scratch(rows, dtype):
    return pltpu.VMEM((D_MODEL // LANES, rows, LANES), dtype)


def _stage_lane_tiles(stage, v):
    for j in range(D_MODEL // LANES):
        stage[j] = v[:, j * LANES:(j + 1) * LANES]


def _strided_rows(stage, start, size, stride):
    return jnp.concatenate([stage[j, pl.ds(start, size, stride=stride), :]
                            for j in range(D_MODEL // LANES)], axis=1)


def _qkv_body(x_ref, wq_ref, wk_ref, wv_ref, o_ref, xs, *stage, r, n):
    if r == 1:
        xs[...] = x_ref[...].astype(BF16)
    else:
        _stage_lane_tiles(stage[0], x_ref[...])
        for c in range(r):
            xs[c * n:(c + 1) * n, :] = _strided_rows(stage[0], c, n, r).astype(BF16)
    for j, w_ref in enumerate((wq_ref, wk_ref, wv_ref)):
        res = jnp.dot(xs[...], w_ref[...], preferred_element_type=F32)
        o_ref[:, :, j * D_MODEL:(j + 1) * D_MODEL] = res.reshape(r, n, D_MODEL).astype(BF16)


def _qkv_group(x, w, layer, g, r):
    n_seq, seq_len, _ = x.shape
    bm = 512
    n = bm // r

    def w_spec(which):
        return pl.BlockSpec((None, D_MODEL, D_MODEL), lambda b, i: (layer, 0, which * N_GROUPS + g),
                            pipeline_mode=pl.Buffered(1))

    return pl.pallas_call(
        functools.partial(_qkv_body, r=r, n=n),
        grid=(n_seq, seq_len // bm),
        in_specs=[pl.BlockSpec((None, bm, D_MODEL), lambda b, i: (b, i, 0)), w_spec(0), w_spec(1), w_spec(2)],
        out_specs=pl.BlockSpec((None, r, n, 3 * D_MODEL), lambda b, i: (b, 0, i, 0)),
        out_shape=jax.ShapeDtypeStruct((n_seq, r, seq_len // r, 3 * D_MODEL), BF16),
        scratch_shapes=[pltpu.VMEM((bm, D_MODEL), BF16)] + ([] if r == 1 else [_lane_tile_scratch(bm, F32)]),
        compiler_params=_params("parallel", "parallel"),
        name=f"qkv_r{r}",
    )(x, w, w, w)


def _attn_body(q_ref, kp_ref, kc_ref, kn_ref, vp_ref, vc_ref, vn_ref, bias_ref,
               o_ref, lse_ref, kbuf, vbuf, *, tile, n_tiles):
    i = pl.program_id(2)
    n_seqs = q_ref.shape[0]
    for u in range(n_seqs):
        kbuf[u, 0:HALF, :] = kp_ref[u]
        kbuf[u, HALF:HALF + tile, :] = kc_ref[u]
        kbuf[u, HALF + tile:, :] = kn_ref[u]
        for hp in range(N_HEADS // 2):
            src = slice(hp * LANES, (hp + 1) * LANES)
            dst = slice(2 * hp * LANES, (2 * hp + 1) * LANES)
            vbuf[u, 0:HALF, dst] = vp_ref[u, :, src]
            vbuf[u, HALF:HALF + tile, dst] = vc_ref[u, :, src]
            vbuf[u, HALF + tile:, dst] = vn_ref[u, :, src]

    @pl.when((pl.program_id(0) == 0) & (pl.program_id(1) == 0) & (i == 0))
    def _():
        for u in range(n_seqs):
            for hp in range(N_HEADS // 2):
                vbuf[u, :, (2 * hp + 1) * LANES:(2 * hp + 2) * LANES] = jnp.ones((tile + 2 * HALF, LANES), BF16)

    lane = lax.broadcasted_iota(jnp.int32, (QT, LANES), 1)
    first_head = lane < HEAD_DIM
    n_sub = tile // QT
    at_first_tile = (i == 0).astype(jnp.int32)
    at_last_tile = (i == n_tiles - 1).astype(jnp.int32)

    for u in range(n_seqs):
        for t in range(n_sub):
            rows = slice(t * QT, (t + 1) * QT)
            keys = slice(t * QT, t * QT + KW)
            variant = (at_first_tile if t == 0 else 0) + 2 * (at_last_tile if t == n_sub - 1 else 0)
            lse_tile = jnp.zeros((QT, LANES), F32)
            for hp in range(N_HEADS // 2):
                cols = slice(hp * LANES, (hp + 1) * LANES)
                q2 = q_ref[u, rows, cols]
                kw = kbuf[u, keys, cols]
                v1 = vbuf[u, keys, 2 * hp * LANES:(2 * hp + 2) * LANES]
                zero = jnp.zeros_like(q2)
                q_pair = jnp.concatenate([jnp.where(first_head, q2, zero), jnp.where(first_head, zero, q2)],
                                         axis=0)
                s = lax.dot_general(q_pair, kw, (((1,), (1,)), ((), ())),
                                    preferred_element_type=F32) + bias_ref[variant, hp]
                m = jnp.max(s, axis=-1, keepdims=True)
                p = jnp.exp2(s - m).astype(BF16)
                pv = jnp.dot(p, v1, preferred_element_type=F32)
                out = jnp.where(first_head, pv[:QT, :LANES], pv[QT:, :LANES])
                den = jnp.where(first_head, pv[:QT, LANES:], pv[QT:, LANES:])
                o_ref[u, rows, cols] = (out * (1.0 / den)).astype(BF16)
                lse = jnp.where(first_head, m[:QT], m[QT:]) * LN2 + jnp.log(den)
                lse_tile = jnp.where((lane & (HEAD_DIM - 1)) == hp, lse, lse_tile)
            lse_ref[u, rows, :] = lse_tile


def _attention_group(qkv, bias_t):
    n_seq, r, sub_len, _ = qkv.shape
    tile = min(ATT_ROWS, sub_len)
    assert sub_len % tile == 0 and tile % QT == 0
    n_tiles = sub_len // tile
    cs = min(r, ATT_ROWS // tile)
    hb = tile // HALF
    last_hb = sub_len // HALF - 1

    def cur(which, width=D_MODEL):
        return pl.BlockSpec((None, cs, tile, width), lambda b, c, i: (b, c, i, which))

    def prev(which):
        return pl.BlockSpec((None, cs, HALF, D_MODEL),
                            lambda b, c, i: (b, c, jnp.maximum(i * hb - 1, 0), which))

    def nxt(which):
        return pl.BlockSpec((None, cs, HALF, D_MODEL),
                            lambda b, c, i: (b, c, jnp.minimum((i + 1) * hb, last_hb), which))

    return pl.pallas_call(
        functools.partial(_attn_body, tile=tile, n_tiles=n_tiles),
        grid=(n_seq, r // cs, n_tiles),
        in_specs=[cur(0), prev(1), cur(1), nxt(1), prev(2), cur(2), nxt(2),
                  _const_spec((4, N_HEADS // 2, 2 * QT, KW))],
        out_specs=[cur(0), cur(0, LANES)],
        out_shape=[jax.ShapeDtypeStruct((n_seq, r, sub_len, D_MODEL), BF16),
                   jax.ShapeDtypeStruct((n_seq, r, sub_len, LANES), F32)],
        scratch_shapes=[pltpu.VMEM((cs, tile + 2 * HALF, D_MODEL), BF16),
                        pltpu.VMEM((cs, tile + 2 * HALF, 2 * D_MODEL), BF16)],
        compiler_params=_params("arbitrary", "arbitrary", "arbitrary"),
        name=f"attn_r{r}",
    )(qkv, qkv, qkv, qkv, qkv, qkv, qkv, bias_t)


def _rel_bucket_np(rel):
    nb = N_BUCKETS // 2
    max_exact = nb // 2
    ret = np.where(rel > 0, nb, 0)
    n = np.abs(rel)
    nf = np.maximum(n, 1).astype(np.float32)
    large = max_exact + (np.log(nf / np.float32(max_exact)) / np.float32(math.log(MAX_DISTANCE / max_exact))
                         * np.float32(nb - max_exact)).astype(np.int32)
    large = np.minimum(large, nb - 1)
    return ret + np.where(n < max_exact, n, large)


def _band_bias(rel_bias):
    kk = np.arange(KW)[None, :]
    tables = []
    for g, (_, r) in enumerate(DIL_GROUPS):
        onehot = np.zeros((N_BUCKETS, 2 * HALF + 1), np.float32)
        onehot[_rel_bucket_np(np.arange(-HALF, HALF + 1) * r), np.arange(2 * HALF + 1)] = 1.0
        vals = jnp.dot(rel_bias[:, g * N_HEADS:(g + 1) * N_HEADS].astype(F32).T, onehot,
                       precision=lax.Precision.HIGHEST)
        period = jnp.concatenate([vals, jnp.full((N_HEADS, KW - 2 * HALF), NEG_INF, F32)], axis=1)
        band = jnp.broadcast_to(period[:, None, :], (N_HEADS, QT, KW + 1)).reshape(N_HEADS, QT * (KW + 1))
        band = band[:, :QT * KW].reshape(N_HEADS, QT, KW)
        variants = []
        for v in range(4):
            ok = np.ones((1, KW), bool)
            if v & 1:
                ok &= kk >= HALF
            if v & 2:
                ok &= kk < KW - HALF
            variants.append(jnp.where(ok[None], band, NEG_INF))
        tables.append((jnp.stack(variants) * LOG2E).reshape(4, N_HEADS // 2, 2 * QT, KW))
    return tables


def _split3(x):
    hi = x.astype(BF16)
    rest = x - hi.astype(F32)
    mid = rest.astype(BF16)
    lo = (rest - mid.astype(F32)).astype(BF16)
    return hi, mid, lo


def _merge_proj_body(o0_ref, o1_ref, o2_ref, l0_ref, l1_ref, l2_ref, x_ref, p1_ref, p2_ref,
                     e_ref, w_ref, g_ref, b_ref, out_ref, *, bm):
    def natural(ref, p_ref, k, width, exact_f32):
        n = bm // ref.shape[0]
        v = ref[:, k * n:(k + 1) * n, :].reshape(bm, width)
        if p_ref is None:
            return v.astype(F32)
        perm = p_ref[...]
        if not exact_f32:
            return jnp.dot(perm, v, preferred_element_type=F32)
        moved = jnp.dot(perm, jnp.concatenate(_split3(v), axis=1), preferred_element_type=F32)
        return moved[:, :width] + moved[:, width:2 * width] + moved[:, 2 * width:]

    perms = (None, p1_ref, p2_ref)
    for k in range(x_ref.shape[0] // bm):
        lses = [natural(ref, p, k, LANES, True) for ref, p in zip((l0_ref, l1_ref, l2_ref), perms)]
        top = jnp.maximum(jnp.maximum(lses[0], lses[1]), lses[2])
        es = [jnp.exp(l - top) for l in lses]
        inv = 1.0 / (es[0] + es[1] + es[2])
        o = jnp.zeros((bm, D_MODEL), F32)
        for ref, p, e in zip((o0_ref, o1_ref, o2_ref), perms, es):
            hi, mid, _ = _split3(e * inv)
            wide = jnp.dot(jnp.concatenate([hi, mid], axis=1), e_ref[...], preferred_element_type=F32)
            o = o + wide * natural(ref, p, k, D_MODEL, False)
        h = jnp.dot(o.astype(BF16), w_ref[...], preferred_element_type=F32)
        rows = slice(k * bm, (k + 1) * bm)
        out_ref[rows, :] = _layer_norm(ALPHA * x_ref[rows, :] + h, g_ref[...], b_ref[...])


def _perm_matrix(bm, r):
    n = bm // r
    p = np.arange(bm)
    m = np.zeros((bm, bm), np.float32)
    m[p, (p % r) * n + p // r] = 1.0
    return jnp.asarray(m, dtype=BF16)


def _merge_proj(outs, lses, x, w, g, b, mixer_layer, layer):
    n_seq, seq_len, _ = x.shape
    bm = MXU_DIM
    step = 2 * bm
    specs = []
    for width in (D_MODEL, LANES):
        for _, r in DIL_GROUPS:
            specs.append(pl.BlockSpec((None, r, step // r, width), lambda b_, i: (b_, 0, i, 0)))
    row = pl.BlockSpec((None, step, D_MODEL), lambda b_, i: (b_, i, 0))
    expand = np.zeros((2 * LANES, D_MODEL), np.float32)
    for h in range(N_HEADS):
        lane = HEAD_DIM * (h % 2) + h // 2
        expand[[lane, LANES + lane], h * HEAD_DIM:(h + 1) * HEAD_DIM] = 1.0
    vec = _layer_spec((1, D_MODEL), layer)
    return pl.pallas_call(
        functools.partial(_merge_proj_body, bm=bm),
        grid=(n_seq, seq_len // step),
        in_specs=specs + [row, _const_spec((bm, bm)), _const_spec((bm, bm)),
                          _const_spec((2 * LANES, D_MODEL)), _layer_spec((D_MODEL, D_MODEL), mixer_layer),
                          vec, vec],
        out_specs=row,
        out_shape=jax.ShapeDtypeStruct((n_seq, seq_len, D_MODEL), F32),
        compiler_params=_params("parallel", "parallel"),
        name="merge_proj_ln",
    )(*outs, *lses, x, _perm_matrix(bm, DIL_GROUPS[1][1]), _perm_matrix(bm, DIL_GROUPS[2][1]),
      jnp.asarray(expand, dtype=BF16), w, g, b)


def _ffn_rows(x, wg_ref, wu_ref, wd_ref, g, b):
    xb = x.astype(BF16)
    acc = jnp.zeros(x.shape, F32)
    for c in range(D_FF // FF_CHUNK):
        cols = slice(c * FF_CHUNK, (c + 1) * FF_CHUNK)
        gate = jnp.dot(xb, wg_ref[:, cols], preferred_element_type=F32)
        up = jnp.dot(xb, wu_ref[:, cols], preferred_element_type=F32)
        h = (gate * jax.nn.sigmoid(gate) * up).astype(BF16)
        acc = acc + jnp.dot(h, wd_ref[cols, :], preferred_element_type=F32)
    return _layer_norm(ALPHA * x + acc, g, b)


def _ffn_body(x_ref, wg_ref, wu_ref, wd_ref, g_ref, b_ref, out_ref):
    out_ref[...] = _ffn_rows(x_ref[...], wg_ref, wu_ref, wd_ref, g_ref[...], b_ref[...])


def _ffn_digit_major_body(x_ref, wg_ref, wu_ref, wd_ref, g_ref, b_ref, out_ref, x2, *, n1):
    _stage_lane_tiles(x2, x_ref[...].reshape(n1 * DFT_SLAB, D_MODEL))
    per_pass = FF_ROWS // n1
    for h in range(DFT_SLAB // per_pass):
        x = jnp.concatenate([_strided_rows(x2, t, n1, DFT_SLAB)
                             for t in range(h * per_pass, (h + 1) * per_pass)], axis=0)
        out_ref[h * FF_ROWS:(h + 1) * FF_ROWS, :] = _ffn_rows(x, wg_ref, wu_ref, wd_ref,
                                                               g_ref[...], b_ref[...])


def _ffn(x, wg, wu, wd, g, b, layer, digit_major):
    vec = _layer_spec((1, D_MODEL), layer)
    weights = [_layer_spec((D_MODEL, D_FF), layer), _layer_spec((D_MODEL, D_FF), layer),
               _layer_spec((D_FF, D_MODEL), layer)]
    if digit_major:
        n_seq, n1, n2, _ = x.shape
        bm = n1 * DFT_SLAB
        body = functools.partial(_ffn_digit_major_body, n1=n1)
        x_spec = pl.BlockSpec((None, n1, DFT_SLAB, D_MODEL), lambda b_, i: (b_, 0, i, 0))
        seq_len = n1 * n2
        scratch = [_lane_tile_scratch(bm, F32)]
    else:
        n_seq, seq_len, _ = x.shape
        bm = FF_ROWS
        body = _ffn_body
        x_spec = pl.BlockSpec((None, bm, D_MODEL), lambda b_, i: (b_, i, 0))
        scratch = []
    return pl.pallas_call(
        body,
        grid=(n_seq, seq_len // bm),
        in_specs=[x_spec] + weights + [vec, vec],
        out_specs=pl.BlockSpec((None, bm, D_MODEL), lambda b_, i: (b_, i, 0)),
        out_shape=jax.ShapeDtypeStruct((n_seq, seq_len, D_MODEL), F32),
        scratch_shapes=scratch,
        compiler_params=_params("parallel", "parallel"),
        name="swiglu_ln_digit_major" if digit_major else "swiglu_ln",
    )(x, wg, wu, wd, g, b)


def _dft_factors(seq_len):
    bits = seq_len.bit_length() - 1
    assert 1 << bits == seq_len
    n1 = 1 << ((bits + 1) // 2)
    return n1, seq_len // n1


def _dft_tables(seq_len):
    n1, n2 = _dft_factors(seq_len)
    c = np.arange(FGROUP)
    ang = 2 * np.pi * ((c[:, None] * c[None, :]) % FGROUP) / FGROUP
    wc = np.concatenate([np.cos(ang), np.sin(ang)], axis=1) / math.sqrt(FGROUP)
    k1 = np.arange(n1)
    ang = 2 * np.pi * ((k1[:, None] * k1[None, :]) % n1) / n1
    w1 = np.block([[np.cos(ang), -np.sin(ang)], [np.sin(ang), np.cos(ang)]])
    k = k1[:, None, None] + n1 * np.arange(n2)[None, :, None]
    ang = 2 * np.pi * ((k * np.arange(n2)[None, None, :]) % seq_len) / seq_len
    m2 = np.concatenate([np.cos(ang), -np.sin(ang)], axis=2) / math.sqrt(seq_len)
    return tuple(jnp.asarray(t, dtype=BF16) for t in (wc, w1, m2))


def _bf16_bits(x):
    u = lax.bitcast_convert_type(x, U32)
    return u + jnp.uint32(0x7FFF) + ((u >> 16) & jnp.uint32(1))


def _pack_complex(re, im):
    return (_bf16_bits(re) >> 16) | (_bf16_bits(im) & jnp.uint32(0xFFFF0000))


def _unpack_complex(word):
    re = lax.bitcast_convert_type(word << 16, F32)
    im = lax.bitcast_convert_type(word & jnp.uint32(0xFFFF0000), F32)
    return re.astype(BF16), im.astype(BF16)


def _dft_front_body(x_ref, wc_ref, w1_ref, a_ref, x2, xb, zbuf, *, n1):
    _stage_lane_tiles(x2, x_ref[...].reshape(n1 * DFT_SLAB, D_MODEL))
    for t in range(DFT_SLAB):
        xb[t * n1:(t + 1) * n1, :] = _strided_rows(x2, t, n1, DFT_SLAB).astype(BF16)
    for g in range(N_FGROUPS):
        cols = slice(g * FGROUP, (g + 1) * FGROUP)
        y = jnp.dot(xb[:, cols], wc_ref[...], preferred_element_type=F32)
        for t in range(DFT_SLAB):
            rows = slice(t * n1, (t + 1) * n1)
            zbuf[t, 0:n1, cols] = y[rows, :FGROUP].astype(BF16)
            zbuf[t, n1:, cols] = y[rows, FGROUP:].astype(BF16)
    for t in range(DFT_SLAB):
        a = jnp.dot(w1_ref[...], zbuf[t], preferred_element_type=F32)
        a_ref[t] = _pack_complex(a[:n1], a[n1:])


def _dft_back_body(a_ref, x_ref, m_ref, w_ref, bo_ref, g_ref, b_ref, out_ref, a2, x2, fbuf,
                   *, n2, kb):
    _stage_lane_tiles(a2, a_ref[...].reshape(n2 * kb, D_MODEL))
    _stage_lane_tiles(x2, x_ref[...].reshape(n2 * kb, D_MODEL))
    half = kb // 2
    for q0 in range(0, kb, half):
        for q in range(q0, q0 + half):
            re, im = _unpack_complex(_strided_rows(a2, q, n2, kb))
            f = jnp.dot(m_ref[q], jnp.concatenate([re, im], axis=0), preferred_element_type=F32)
            fbuf[q * n2:(q + 1) * n2, :] = f.astype(BF16)
        h = jnp.dot(fbuf[q0 * n2:(q0 + half) * n2, :], w_ref[...], preferred_element_type=F32) + bo_ref[...]
        for q in range(q0, q0 + half):
            y = ALPHA * _strided_rows(x2, q, n2, kb) + h[(q - q0) * n2:(q - q0 + 1) * n2]
            out_ref[q] = _layer_norm(y, g_ref[...], b_ref[...])


def _fourier_mixer(x, w, bo, g, b, mixer_layer, layer):
    n_seq, seq_len, _ = x.shape
    n1, n2 = _dft_factors(seq_len)
    wc, w1, m2 = _dft_tables(seq_len)
    sb = DFT_SLAB
    a = pl.pallas_call(
        functools.partial(_dft_front_body, n1=n1),
        grid=(n_seq, n2 // sb),
        in_specs=[pl.BlockSpec((None, n1, sb, D_MODEL), lambda b_, j: (b_, 0, j, 0)),
                  _const_spec((FGROUP, 2 * FGROUP)), _const_spec((2 * n1, 2 * n1))],
        out_specs=pl.BlockSpec((None, sb, n1, D_MODEL), lambda b_, j: (b_, j, 0, 0)),
        out_shape=jax.ShapeDtypeStruct((n_seq, n2, n1, D_MODEL), U32),
        scratch_shapes=[_lane_tile_scratch(sb * n1, F32),
                        pltpu.VMEM((sb * n1, D_MODEL), BF16),
                        pltpu.VMEM((sb, 2 * n1, D_MODEL), BF16)],
        compiler_params=_params("parallel", "parallel"),
        name="dft_front",
    )(x.reshape(n_seq, n1, n2, D_MODEL), wc, w1)
    kb = DFT_SLAB
    slab = pl.BlockSpec((None, n2, kb, D_MODEL), lambda b_, k: (b_, 0, k, 0))
    m_spec = pl.BlockSpec((kb, n2, 2 * n2), lambda b_, k: (k, 0, 0))
    vec = _layer_spec((1, D_MODEL), layer)
    return pl.pallas_call(
        functools.partial(_dft_back_body, n2=n2, kb=kb),
        grid=(n_seq, n1 // kb),
        in_specs=[slab, slab, m_spec, _layer_spec((D_MODEL, D_MODEL), mixer_layer),
                  _layer_spec((1, D_MODEL), mixer_layer), vec, vec],
        out_specs=pl.BlockSpec((None, kb, n2, D_MODEL), lambda b_, k: (b_, k, 0, 0)),
        out_shape=jax.ShapeDtypeStruct((n_seq, n1, n2, D_MODEL), F32),
        scratch_shapes=[_lane_tile_scratch(kb * n2, U32), _lane_tile_scratch(kb * n2, F32),
                        pltpu.VMEM((kb * n2, D_MODEL), BF16)],
        compiler_params=_params("parallel", "parallel"),
        name="dft_back_proj_ln",
    )(a, x.reshape(n_seq, n2, n1, D_MODEL), m2, w, bo, g, b)


def _trunk(x, p):
    for i in range(DEPTH):
        li = i // 2
        mixer_a = i % 2 == 0
        if mixer_a:
            outs, lses = [], []
            for g, (_, r) in enumerate(DIL_GROUPS):
                o, lse = _attention_group(_qkv_group(x, p["w_qkv"], li, g, r), p["band_bias"][g])
                outs.append(o)
                lses.append(lse)
            x = _merge_proj(outs, lses, x, p["w_o_a"], p["ln1_g"], p["ln1_b"], li, i)
        else:
            x = _fourier_mixer(x, p["w_o_b"], p["b_o_b"], p["ln1_g"], p["ln1_b"], li, i)
        x = _ffn(x, p["w_gate"], p["w_up"], p["w_down"], p["ln2_g"], p["ln2_b"], i,
                 digit_major=not mixer_a)
    return x


def kernel(x_prompt, x_sample, rel_bias, w_qkv_a, w_o_a, w_o_b, b_o_b, w_gate, w_up, w_down,
           ln1_g, ln1_b, ln2_g, ln2_b):
    q_cols = w_qkv_a.shape[-1] // 3
    col_scale = jnp.concatenate([jnp.full((q_cols,), LOG2E * HEAD_DIM ** -0.5, F32),
                                 jnp.ones((2 * q_cols,), F32)])
    vec = lambda a: a.astype(F32)[:, None, :]
    p = {
        "w_qkv": (w_qkv_a * col_scale).astype(BF16),
        "w_o_a": w_o_a.astype(BF16),
        "w_o_b": w_o_b.astype(BF16),
        "b_o_b": vec(b_o_b),
        "w_gate": w_gate.astype(BF16),
        "w_up": w_up.astype(BF16),
        "w_down": w_down.astype(BF16),
        "ln1_g": vec(ln1_g), "ln1_b": vec(ln1_b), "ln2_g": vec(ln2_g), "ln2_b": vec(ln2_b),
        "band_bias": _band_bias(rel_bias),
    }
    return _trunk(x_prompt, p), _trunk(x_sample, p)
```

```python
import functools
import math

import numpy as np
import jax
import jax.numpy as jnp
from jax import lax
from jax.experimental import pallas as pl
from jax.experimental.pallas import tpu as pltpu

D_MODEL = 1024
DEPTH = 4
HEAD_DIM = 64
N_HEADS = 16
DIL_GROUPS = ((128, 1), (512, 4), (2048, 16))
N_GROUPS = len(DIL_GROUPS)
N_BUCKETS = 32
MAX_DISTANCE = 1024
N_FGROUPS = 4
FGROUP = D_MODEL // N_FGROUPS
D_FF = 2816
ALPHA = (2 * DEPTH) ** 0.25
LN_EPS = 1e-5
NEG_INF = -1e30
LOG2E = math.log2(math.e)
LN2 = math.log(2.0)

HALF = 64
QT = 128
KW = QT + 2 * HALF
ATT_ROWS = 1024
LANES = 128
MXU_DIM = 256
FF_CHUNK = MXU_DIM
FF_ROWS = 512
DFT_SLAB = 8
VMEM_LIMIT = 52 * 2 ** 20

BF16 = jnp.bfloat16
F32 = jnp.float32
U32 = jnp.uint32


def _params(*sem):
    return pltpu.CompilerParams(dimension_semantics=sem, vmem_limit_bytes=VMEM_LIMIT)


def _const_spec(shape):
    nd = len(shape)
    return pl.BlockSpec(shape, lambda *_: (0,) * nd, pipeline_mode=pl.Buffered(1))


def _layer_spec(shape, layer):
    nd = len(shape)
    return pl.BlockSpec((None,) + tuple(shape), lambda *_: (layer,) + (0,) * nd,
                        pipeline_mode=pl.Buffered(1))


def _layer_norm(y, g, b):
    mu = jnp.mean(y, axis=-1, keepdims=True)
    yc = y - mu
    var = jnp.mean(yc * yc, axis=-1, keepdims=True)
    return yc * lax.rsqrt(var + LN_EPS) * g + b


def _lane_tile_scratch(rows, dtype):
    return pltpu.VMEM((D_MODEL // LANES, rows, LANES), dtype)


def _stage_lane_tiles(stage, v):
    for j in range(D_MODEL // LANES):
        stage[j] = v[:, j * LANES:(j + 1) * LANES]


def _strided_rows(stage, start, size, stride):
    return jnp.concatenate([stage[j, pl.ds(start, size, stride=stride), :]
                            for j in range(D_MODEL // LANES)], axis=1)


def _qkv_body(x_ref, wq_ref, wk_ref, wv_ref, o_ref, xs, *stage, r, n):
    if r == 1:
        xs[...] = x_ref[...].astype(BF16)
    else:
        _stage_lane_tiles(stage[0], x_ref[...])
        for c in range(r):
            xs[c * n:(c + 1) * n, :] = _strided_rows(stage[0], c, n, r).astype(BF16)
    for j, w_ref in enumerate((wq_ref, wk_ref, wv_ref)):
        res = jnp.dot(xs[...], w_ref[...], preferred_element_type=F32)
        o_ref[:, :, j * D_MODEL:(j + 1) * D_MODEL] = res.reshape(r, n, D_MODEL).astype(BF16)


def _qkv_group(x, w, layer, g, r):
    n_seq, seq_len, _ = x.shape
    bm = 512
    n = bm // r

    def w_spec(which):
        return pl.BlockSpec((None, D_MODEL, D_MODEL), lambda b, i: (layer, 0, which * N_GROUPS + g),
                            pipeline_mode=pl.Buffered(1))

    return pl.pallas_call(
        functools.partial(_qkv_body, r=r, n=n),
        grid=(n_seq, seq_len // bm),
        in_specs=[pl.BlockSpec((None, bm, D_MODEL), lambda b, i: (b, i, 0)), w_spec(0), w_spec(1), w_spec(2)],
        out_specs=pl.BlockSpec((None, r, n, 3 * D_MODEL), lambda b, i: (b, 0, i, 0)),
        out_shape=jax.ShapeDtypeStruct((n_seq, r, seq_len // r, 3 * D_MODEL), BF16),
        scratch_shapes=[pltpu.VMEM((bm, D_MODEL), BF16)] + ([] if r == 1 else [_lane_tile_scratch(bm, F32)]),
        compiler_params=_params("parallel", "parallel"),
        name=f"qkv_r{r}",
    )(x, w, w, w)


def _attn_body(q_ref, kp_ref, kc_ref, kn_ref, vp_ref, vc_ref, vn_ref, bias_ref,
               o_ref, lse_ref, vbuf, *, tile, n_tiles):
    i = pl.program_id(2)
    n_seqs = q_ref.shape[0]

    def key_window(u, t, cols):
        lo, hi = t * QT - HALF, t * QT + QT + HALF
        parts = [kp_ref[u, :, cols]] if lo < 0 else []
        parts.append(kc_ref[u, max(lo, 0):min(hi, tile), cols])
        if hi > tile:
            parts.append(kn_ref[u, :, cols])
        return parts[0] if len(parts) == 1 else jnp.concatenate(parts, axis=0)

    def fill_values(u, lo, hi):
        segments = ((0, HALF, vp_ref), (HALF, HALF + tile, vc_ref), (HALF + tile, tile + 2 * HALF, vn_ref))
        for start, stop, ref in segments:
            a, b = max(lo, start), min(hi, stop)
            if a < b:
                for hp in range(N_HEADS // 2):
                    vbuf[u, a:b, 2 * hp * LANES:(2 * hp + 1) * LANES] = (
                        ref[u, a - start:b - start, hp * LANES:(hp + 1) * LANES])

    @pl.when((pl.program_id(0) == 0) & (pl.program_id(1) == 0) & (i == 0))
    def _():
        for u in range(n_seqs):
            for hp in range(N_HEADS // 2):
                vbuf[u, :, (2 * hp + 1) * LANES:(2 * hp + 2) * LANES] = jnp.ones((tile + 2 * HALF, LANES), BF16)

    lane = lax.broadcasted_iota(jnp.int32, (QT, LANES), 1)
    first_head = lane < HEAD_DIM
    n_sub = tile // QT
    at_first_tile = (i == 0).astype(jnp.int32)
    at_last_tile = (i == n_tiles - 1).astype(jnp.int32)

    for u in range(n_seqs):
        for t in range(n_sub):
            rows = slice(t * QT, (t + 1) * QT)
            keys = slice(t * QT, t * QT + KW)
            fill_values(u, 0 if t == 0 else keys.stop - QT, keys.stop)
            variant = (at_first_tile if t == 0 else 0) + 2 * (at_last_tile if t == n_sub - 1 else 0)
            lse_tile = jnp.zeros((QT, LANES), F32)
            for hp in range(N_HEADS // 2):
                cols = slice(hp * LANES, (hp + 1) * LANES)
                q2 = q_ref[u, rows, cols]
                kw = key_window(u, t, cols)
                v1 = vbuf[u, keys, 2 * hp * LANES:(2 * hp + 2) * LANES]
                zero = jnp.zeros_like(q2)
                q_pair = jnp.concatenate([jnp.where(first_head, q2, zero), jnp.where(first_head, zero, q2)],
                                         axis=0)
                s = lax.dot_general(q_pair, kw, (((1,), (1,)), ((), ())),
                                    preferred_element_type=F32) + bias_ref[variant, hp]
                m = jnp.max(s, axis=-1, keepdims=True)
                p = jnp.exp2(s - m).astype(BF16)
                pv = jnp.dot(p, v1, preferred_element_type=F32)
                out = jnp.where(first_head, pv[:QT, :LANES], pv[QT:, :LANES])
                den = jnp.where(first_head, pv[:QT, LANES:], pv[QT:, LANES:])
                o_ref[u, rows, cols] = (out * (1.0 / den)).astype(BF16)
                lse = jnp.where(first_head, m[:QT], m[QT:]) * LN2 + jnp.log(den)
                lse_tile = jnp.where((lane & (HEAD_DIM - 1)) == hp, lse, lse_tile)
            lse_ref[u, rows, :] = lse_tile


def _attention_group(qkv, bias_t):
    n_seq, r, sub_len, _ = qkv.shape
    tile = min(ATT_ROWS, sub_len)
    assert sub_len % tile == 0 and tile % QT == 0
    n_tiles = sub_len // tile
    cs = min(r, ATT_ROWS // tile)
    hb = tile // HALF
    last_hb = sub_len // HALF - 1

    def cur(which, width=D_MODEL):
        return pl.BlockSpec((None, cs, tile, width), lambda b, c, i: (b, c, i, which))

    def prev(which):
        return pl.BlockSpec((None, cs, HALF, D_MODEL),
                            lambda b, c, i: (b, c, jnp.maximum(i * hb - 1, 0), which))

    def nxt(which):
        return pl.BlockSpec((None, cs, HALF, D_MODEL),
                            lambda b, c, i: (b, c, jnp.minimum((i + 1) * hb, last_hb), which))

    return pl.pallas_call(
        functools.partial(_attn_body, tile=tile, n_tiles=n_tiles),
        grid=(n_seq, r // cs, n_tiles),
        in_specs=[cur(0), prev(1), cur(1), nxt(1), prev(2), cur(2), nxt(2),
                  _const_spec((4, N_HEADS // 2, 2 * QT, KW))],
        out_specs=[cur(0), cur(0, LANES)],
        out_shape=[jax.ShapeDtypeStruct((n_seq, r, sub_len, D_MODEL), BF16),
                   jax.ShapeDtypeStruct((n_seq, r, sub_len, LANES), F32)],
        scratch_shapes=[pltpu.VMEM((cs, tile + 2 * HALF, 2 * D_MODEL), BF16)],
        compiler_params=_params("arbitrary", "arbitrary", "arbitrary"),
        name=f"attn_r{r}",
    )(qkv, qkv, qkv, qkv, qkv, qkv, qkv, bias_t)


def _rel_bucket_np(rel):
    nb = N_BUCKETS // 2
    max_exact = nb // 2
    ret = np.where(rel > 0, nb, 0)
    n = np.abs(rel)
    nf = np.maximum(n, 1).astype(np.float32)
    large = max_exact + (np.log(nf / np.float32(max_exact)) / np.float32(math.log(MAX_DISTANCE / max_exact))
                         * np.float32(nb - max_exact)).astype(np.int32)
    large = np.minimum(large, nb - 1)
    return ret + np.where(n < max_exact, n, large)


def _band_bias(rel_bias):
    kk = np.arange(KW)[None, :]
    tables = []
    for g, (_, r) in enumerate(DIL_GROUPS):
        onehot = np.zeros((N_BUCKETS, 2 * HALF + 1), np.float32)
        onehot[_rel_bucket_np(np.arange(-HALF, HALF + 1) * r), np.arange(2 * HALF + 1)] = 1.0
        vals = jnp.dot(rel_bias[:, g * N_HEADS:(g + 1) * N_HEADS].astype(F32).T, onehot,
                       precision=lax.Precision.HIGHEST)
        period = jnp.concatenate([vals, jnp.full((N_HEADS, KW - 2 * HALF), NEG_INF, F32)], axis=1)
        band = jnp.broadcast_to(period[:, None, :], (N_HEADS, QT, KW + 1)).reshape(N_HEADS, QT * (KW + 1))
        band = band[:, :QT * KW].reshape(N_HEADS, QT, KW)
        variants = []
        for v in range(4):
            ok = np.ones((1, KW), bool)
            if v & 1:
                ok &= kk >= HALF
            if v & 2:
                ok &= kk < KW - HALF
            variants.append(jnp.where(ok[None], band, NEG_INF))
        tables.append((jnp.stack(variants) * LOG2E).reshape(4, N_HEADS // 2, 2 * QT, KW))
    return tables


def _split3(x):
    hi = x.astype(BF16)
    rest = x - hi.astype(F32)
    mid = rest.astype(BF16)
    lo = (rest - mid.astype(F32)).astype(BF16)
    return hi, mid, lo


def _merge_proj_body(o0_ref, o1_ref, o2_ref, l0_ref, l1_ref, l2_ref, x_ref, p1_ref, p2_ref,
                     e_ref, w_ref, g_ref, b_ref, out_ref, *, bm):
    def natural(ref, p_ref, k, width, exact_f32):
        n = bm // ref.shape[0]
        v = ref[:, k * n:(k + 1) * n, :].reshape(bm, width)
        if p_ref is None:
            return v.astype(F32)
        perm = p_ref[...]
        if not exact_f32:
            return jnp.dot(perm, v, preferred_element_type=F32)
        moved = jnp.dot(perm, jnp.concatenate(_split3(v), axis=1), preferred_element_type=F32)
        return moved[:, :width] + moved[:, width:2 * width] + moved[:, 2 * width:]

    perms = (None, p1_ref, p2_ref)
    for k in range(x_ref.shape[0] // bm):
        lses = [natural(ref, p, k, LANES, True) for ref, p in zip((l0_ref, l1_ref, l2_ref), perms)]
        top = jnp.maximum(jnp.maximum(lses[0], lses[1]), lses[2])
        es = [jnp.exp(l - top) for l in lses]
        inv = 1.0 / (es[0] + es[1] + es[2])
        o = jnp.zeros((bm, D_MODEL), F32)
        for ref, p, e in zip((o0_ref, o1_ref, o2_ref), perms, es):
            hi, mid, _ = _split3(e * inv)
            wide = jnp.dot(jnp.concatenate([hi, mid], axis=1), e_ref[...], preferred_element_type=F32)
            o = o + wide * natural(ref, p, k, D_MODEL, False)
        h = jnp.dot(o.astype(BF16), w_ref[...], preferred_element_type=F32)
        rows = slice(k * bm, (k + 1) * bm)
        out_ref[rows, :] = _layer_norm(ALPHA * x_ref[rows, :] + h, g_ref[...], b_ref[...])


def _perm_matrix(bm, r):
    n = bm // r
    p = np.arange(bm)
    m = np.zeros((bm, bm), np.float32)
    m[p, (p % r) * n + p // r] = 1.0
    return jnp.asarray(m, dtype=BF16)


def _merge_proj(outs, lses, x, w, g, b, mixer_layer, layer):
    n_seq, seq_len, _ = x.shape
    bm = MXU_DIM
    step = 4 * bm
    specs = []
    for width in (D_MODEL, LANES):
        for _, r in DIL_GROUPS:
            specs.append(pl.BlockSpec((None, r, step // r, width), lambda b_, i: (b_, 0, i, 0)))
    row = pl.BlockSpec((None, step, D_MODEL), lambda b_, i: (b_, i, 0))
    expand = np.zeros((2 * LANES, D_MODEL), np.float32)
    for h in range(N_HEADS):
        lane = HEAD_DIM * (h % 2) + h // 2
        expand[[lane, LANES + lane], h * HEAD_DIM:(h + 1) * HEAD_DIM] = 1.0
    vec = _layer_spec((1, D_MODEL), layer)
    return pl.pallas_call(
        functools.partial(_merge_proj_body, bm=bm),
        grid=(n_seq, seq_len // step),
        in_specs=specs + [row, _const_spec((bm, bm)), _const_spec((bm, bm)),
                          _const_spec((2 * LANES, D_MODEL)), _layer_spec((D_MODEL, D_MODEL), mixer_layer),
                          vec, vec],
        out_specs=row,
        out_shape=jax.ShapeDtypeStruct((n_seq, seq_len, D_MODEL), F32),
        compiler_params=_params("parallel", "parallel"),
        name="merge_proj_ln",
    )(*outs, *lses, x, _perm_matrix(bm, DIL_GROUPS[1][1]), _perm_matrix(bm, DIL_GROUPS[2][1]),
      jnp.asarray(expand, dtype=BF16), w, g, b)


def _ffn_rows(x, wg_ref, wu_ref, wd_ref, g, b):
    xb = x.astype(BF16)
    acc = jnp.zeros(x.shape, F32)
    for c in range(D_FF // FF_CHUNK):
        cols = slice(c * FF_CHUNK, (c + 1) * FF_CHUNK)
        gate = jnp.dot(xb, wg_ref[:, cols], preferred_element_type=F32)
        up = jnp.dot(xb, wu_ref[:, cols], preferred_element_type=F32)
        h = (gate * jax.nn.sigmoid(gate) * up).astype(BF16)
        acc = acc + jnp.dot(h, wd_ref[cols, :], preferred_element_type=F32)
    return _layer_norm(ALPHA * x + acc, g, b)


def _ffn_body(x_ref, wg_ref, wu_ref, wd_ref, g_ref, b_ref, out_ref):
    out_ref[...] = _ffn_rows(x_ref[...], wg_ref, wu_ref, wd_ref, g_ref[...], b_ref[...])


def _ffn_digit_major_body(x_ref, wg_ref, wu_ref, wd_ref, g_ref, b_ref, out_ref, x2, *, n1):
    _stage_lane_tiles(x2, x_ref[...].reshape(n1 * DFT_SLAB, D_MODEL))
    per_pass = FF_ROWS // n1
    for h in range(DFT_SLAB // per_pass):
        x = jnp.concatenate([_strided_rows(x2, t, n1, DFT_SLAB)
                             for t in range(h * per_pass, (h + 1) * per_pass)], axis=0)
        out_ref[h * FF_ROWS:(h + 1) * FF_ROWS, :] = _ffn_rows(x, wg_ref, wu_ref, wd_ref,
                                                               g_ref[...], b_ref[...])


def _ffn(x, wg, wu, wd, g, b, layer, digit_major):
    vec = _layer_spec((1, D_MODEL), layer)
    weights = [_layer_spec((D_MODEL, D_FF), layer), _layer_spec((D_MODEL, D_FF), layer),
               _layer_spec((D_FF, D_MODEL), layer)]
    if digit_major:
        n_seq, n1, n2, _ = x.shape
        bm = n1 * DFT_SLAB
        body = functools.partial(_ffn_digit_major_body, n1=n1)
        x_spec = pl.BlockSpec((None, n1, DFT_SLAB, D_MODEL), lambda b_, i: (b_, 0, i, 0))
        seq_len = n1 * n2
        scratch = [_lane_tile_scratch(bm, F32)]
    else:
        n_seq, seq_len, _ = x.shape
        bm = FF_ROWS
        body = _ffn_body
        x_spec = pl.BlockSpec((None, bm, D_MODEL), lambda b_, i: (b_, i, 0))
        scratch = []
    return pl.pallas_call(
        body,
        grid=(n_seq, seq_len // bm),
        in_specs=[x_spec] + weights + [vec, vec],
        out_specs=pl.BlockSpec((None, bm, D_MODEL), lambda b_, i: (b_, i, 0)),
        out_shape=jax.ShapeDtypeStruct((n_seq, seq_len, D_MODEL), F32),
        scratch_shapes=scratch,
        compiler_params=_params("parallel", "parallel"),
        name="swiglu_ln_digit_major" if digit_major else "swiglu_ln",
    )(x, wg, wu, wd, g, b)


def _dft_factors(seq_len):
    bits = seq_len.bit_length() - 1
    assert 1 << bits == seq_len
    n1 = 1 << ((bits + 1) // 2)
    return n1, seq_len // n1


def _dft_tables(seq_len):
    n1, n2 = _dft_factors(seq_len)
    c = np.arange(FGROUP)
    ang = 2 * np.pi * ((c[:, None] * c[None, :]) % FGROUP) / FGROUP
    wc = np.concatenate([np.cos(ang), np.sin(ang)], axis=1) / math.sqrt(FGROUP)
    k1 = np.arange(n1)
    ang = 2 * np.pi * ((k1[:, None] * k1[None, :]) % n1) / n1
    w1 = np.block([[np.cos(ang), -np.sin(ang)], [np.sin(ang), np.cos(ang)]])
    k = k1[:, None, None] + n1 * np.arange(n2)[None, :, None]
    ang = 2 * np.pi * ((k * np.arange(n2)[None, None, :]) % seq_len) / seq_len
    m2 = np.concatenate([np.cos(ang), -np.sin(ang)], axis=2) / math.sqrt(seq_len)
    return tuple(jnp.asarray(t, dtype=BF16) for t in (wc, w1, m2))


def _bf16_bits(x):
    u = lax.bitcast_convert_type(x, U32)
    return u + jnp.uint32(0x7FFF) + ((u >> 16) & jnp.uint32(1))


def _pack_complex(re, im):
    return (_bf16_bits(re) >> 16) | (_bf16_bits(im) & jnp.uint32(0xFFFF0000))


def _unpack_complex(word):
    re = lax.bitcast_convert_type(word << 16, F32)
    im = lax.bitcast_convert_type(word & jnp.uint32(0xFFFF0000), F32)
    return re.astype(BF16), im.astype(BF16)


def _dft_front_body(x_ref, wc_ref, w1_ref, a_ref, x2, xb, zbuf, *, n1):
    _stage_lane_tiles(x2, x_ref[...].reshape(n1 * DFT_SLAB, D_MODEL))
    for t in range(DFT_SLAB):
        xb[t * n1:(t + 1) * n1, :] = _strided_rows(x2, t, n1, DFT_SLAB).astype(BF16)
    for g in range(N_FGROUPS):
        cols = slice(g * FGROUP, (g + 1) * FGROUP)
        y = jnp.dot(xb[:, cols], wc_ref[...], preferred_element_type=F32)
        for t in range(DFT_SLAB):
            rows = slice(t * n1, (t + 1) * n1)
            zbuf[t, 0:n1, cols] = y[rows, :FGROUP].astype(BF16)
            zbuf[t, n1:, cols] = y[rows, FGROUP:].astype(BF16)
    for t in range(DFT_SLAB):
        a = jnp.dot(w1_ref[...], zbuf[t], preferred_element_type=F32)
        a_ref[t] = _pack_complex(a[:n1], a[n1:])


def _dft_back_body(a_ref, x_ref, m_ref, w_ref, bo_ref, g_ref, b_ref, out_ref, a2, x2, fbuf,
                   *, n2, kb):
    _stage_lane_tiles(a2, a_ref[...].reshape(n2 * kb, D_MODEL))
    _stage_lane_tiles(x2, x_ref[...].reshape(n2 * kb, D_MODEL))
    half = kb // 2
    for q0 in range(0, kb, half):
        for q in range(q0, q0 + half):
            re, im = _unpack_complex(_strided_rows(a2, q, n2, kb))
            f = jnp.dot(m_ref[q], jnp.concatenate([re, im], axis=0), preferred_element_type=F32)
            fbuf[q * n2:(q + 1) * n2, :] = f.astype(BF16)
        h = jnp.dot(fbuf[q0 * n2:(q0 + half) * n2, :], w_ref[...], preferred_element_type=F32) + bo_ref[...]
        for q in range(q0, q0 + half):
            y = ALPHA * _strided_rows(x2, q, n2, kb) + h[(q - q0) * n2:(q - q0 + 1) * n2]
            out_ref[q] = _layer_norm(y, g_ref[...], b_ref[...])


def _fourier_mixer(x, w, bo, g, b, mixer_layer, layer):
    n_seq, seq_len, _ = x.shape
    n1, n2 = _dft_factors(seq_len)
    wc, w1, m2 = _dft_tables(seq_len)
    sb = DFT_SLAB
    a = pl.pallas_call(
        functools.partial(_dft_front_body, n1=n1),
        grid=(n_seq, n2 // sb),
        in_specs=[pl.BlockSpec((None, n1, sb, D_MODEL), lambda b_, j: (b_, 0, j, 0)),
                  _const_spec((FGROUP, 2 * FGROUP)), _const_spec((2 * n1, 2 * n1))],
        out_specs=pl.BlockSpec((None, sb, n1, D_MODEL), lambda b_, j: (b_, j, 0, 0)),
        out_shape=jax.ShapeDtypeStruct((n_seq, n2, n1, D_MODEL), U32),
        scratch_shapes=[_lane_tile_scratch(sb * n1, F32),
                        pltpu.VMEM((sb * n1, D_MODEL), BF16),
                        pltpu.VMEM((sb, 2 * n1, D_MODEL), BF16)],
        compiler_params=_params("parallel", "parallel"),
        name="dft_front",
    )(x.reshape(n_seq, n1, n2, D_MODEL), wc, w1)
    kb = DFT_SLAB
    slab = pl.BlockSpec((None, n2, kb, D_MODEL), lambda b_, k: (b_, 0, k, 0))
    m_spec = pl.BlockSpec((kb, n2, 2 * n2), lambda b_, k: (k, 0, 0))
    vec = _layer_spec((1, D_MODEL), layer)
    return pl.pallas_call(
        functools.partial(_dft_back_body, n2=n2, kb=kb),
        grid=(n_seq, n1 // kb),
        in_specs=[slab, slab, m_spec, _layer_spec((D_MODEL, D_MODEL), mixer_layer),
                  _layer_spec((1, D_MODEL), mixer_layer), vec, vec],
        out_specs=pl.BlockSpec((None, kb, n2, D_MODEL), lambda b_, k: (b_, k, 0, 0)),
        out_shape=jax.ShapeDtypeStruct((n_seq, n1, n2, D_MODEL), F32),
        scratch_shapes=[_lane_tile_scratch(kb * n2, U32), _lane_tile_scratch(kb * n2, F32),
                        pltpu.VMEM((kb * n2, D_MODEL), BF16)],
        compiler_params=_params("parallel", "parallel"),
        name="dft_back_proj_ln",
    )(a, x.reshape(n_seq, n2, n1, D_MODEL), m2, w, bo, g, b)


def _trunk(x, p):
    for i in range(DEPTH):
        li = i // 2
        mixer_a = i % 2 == 0
        if mixer_a:
            outs, lses = [], []
            for g, (_, r) in enumerate(DIL_GROUPS):
                o, lse = _attention_group(_qkv_group(x, p["w_qkv"], li, g, r), p["band_bias"][g])
                outs.append(o)
                lses.append(lse)
            x = _merge_proj(outs, lses, x, p["w_o_a"], p["ln1_g"], p["ln1_b"], li, i)
        else:
            x = _fourier_mixer(x, p["w_o_b"], p["b_o_b"], p["ln1_g"], p["ln1_b"], li, i)
        x = _ffn(x, p["w_gate"], p["w_up"], p["w_down"], p["ln2_g"], p["ln2_b"], i,
                 digit_major=not mixer_a)
    return x


def kernel(x_prompt, x_sample, rel_bias, w_qkv_a, w_o_a, w_o_b, b_o_b, w_gate, w_up, w_down,
           ln1_g, ln1_b, ln2_g, ln2_b):
    q_cols = w_qkv_a.shape[-1] // 3
    col_scale = jnp.concatenate([jnp.full((q_cols,), LOG2E * HEAD_DIM ** -0.5, F32),
                                 jnp.ones((2 * q_cols,), F32)])
    vec = lambda a: a.astype(F32)[:, None, :]
    p = {
        "w_qkv": (w_qkv_a * col_scale).astype(BF16),
        "w_o_a": w_o_a.astype(BF16),
        "w_o_b": w_o_b.astype(BF16),
        "b_o_b": vec(b_o_b),
        "w_gate": w_gate.astype(BF16),
        "w_up": w_up.astype(BF16),
        "w_down": w_down.astype(BF16),
        "ln1_g": vec(ln1_g), "ln1_b": vec(ln1_b), "ln2_g": vec(ln2_g), "ln2_b": vec(ln2_b),
        "band_bias": _band_bias(rel_bias),
    }
    return _trunk(x_prompt, p), _trunk(x_sample, p)
```

```python
import functools
import math

import numpy as np
import jax
import jax.numpy as jnp
from jax import lax
from jax.experimental import pallas as pl
from jax.experimental.pallas import tpu as pltpu

D_MODEL = 1024
DEPTH = 4
HEAD_DIM = 64
N_HEADS = 16
DIL_GROUPS = ((128, 1), (512, 4), (2048, 16))
N_GROUPS = len(DIL_GROUPS)
N_BUCKETS = 32
MAX_DISTANCE = 1024
N_FGROUPS = 4
FGROUP = D_MODEL // N_FGROUPS
D_FF = 2816
ALPHA = (2 * DEPTH) ** 0.25
LN_EPS = 1e-5
NEG_INF = -1e30
LOG2E = math.log2(math.e)
LN2 = math.log(2.0)

HALF = 64
QT = 128
KW = QT + 2 * HALF
ATT_ROWS = 1024
LANES = 128
MXU_DIM = 256
FF_CHUNK = MXU_DIM
FF_ROWS = 512
FF_LAG = 3
DFT_SLAB = 8
VMEM_LIMIT = 52 * 2 ** 20

BF16 = jnp.bfloat16
F32 = jnp.float32
U32 = jnp.uint32


def _params(*sem):
    return pltpu.CompilerParams(dimension_semantics=sem, vmem_limit_bytes=VMEM_LIMIT)


def _const_spec(shape):
    nd = len(shape)
    return pl.BlockSpec(shape, lambda *_: (0,) * nd, pipeline_mode=pl.Buffered(1))


def _layer_spec(shape, layer):
    nd = len(shape)
    return pl.BlockSpec((None,) + tuple(shape), lambda *_: (layer,) + (0,) * nd,
                        pipeline_mode=pl.Buffered(1))


def _layer_norm(y, g, b):
    mu = jnp.mean(y, axis=-1, keepdims=True)
    yc = y - mu
    var = jnp.mean(yc * yc, axis=-1, keepdims=True)
    return yc * lax.rsqrt(var + LN_EPS) * g + b


def _lane_tile_scratch(rows, dtype):
    return pltpu.VMEM((D_MODEL // LANES, rows, LANES), dtype)


def _stage_lane_tiles(stage, v):
    for j in range(D_MODEL // LANES):
        stage[j] = v[:, j * LANES:(j + 1) * LANES]


def _strided_rows(stage, start, size, stride):
    return jnp.concatenate([stage[j, pl.ds(start, size, stride=stride), :]
                            for j in range(D_MODEL // LANES)], axis=1)


def _qkv_body(x_ref, wq_ref, wk_ref, wv_ref, o_ref, xs, *stage, r, n):
    if r == 1:
        xs[...] = x_ref[...].astype(BF16)
    else:
        _stage_lane_tiles(stage[0], x_ref[...])
        for c in range(r):
            xs[c * n:(c + 1) * n, :] = _strided_rows(stage[0], c, n, r).astype(BF16)
    for j, w_ref in enumerate((wq_ref, wk_ref, wv_ref)):
        res = jnp.dot(xs[...], w_ref[...], preferred_element_type=F32)
        o_ref[:, :, j * D_MODEL:(j + 1) * D_MODEL] = res.reshape(r, n, D_MODEL).astype(BF16)


def _qkv_group(x, w, layer, g, r):
    n_seq, seq_len, _ = x.shape
    bm = 512
    n = bm // r

    def w_spec(which):
        return pl.BlockSpec((None, D_MODEL, D_MODEL), lambda b, i: (layer, 0, which * N_GROUPS + g),
                            pipeline_mode=pl.Buffered(1))

    return pl.pallas_call(
        functools.partial(_qkv_body, r=r, n=n),
        grid=(n_seq, seq_len // bm),
        in_specs=[pl.BlockSpec((None, bm, D_MODEL), lambda b, i: (b, i, 0)), w_spec(0), w_spec(1), w_spec(2)],
        out_specs=pl.BlockSpec((None, r, n, 3 * D_MODEL), lambda b, i: (b, 0, i, 0)),
        out_shape=jax.ShapeDtypeStruct((n_seq, r, seq_len // r, 3 * D_MODEL), BF16),
        scratch_shapes=[pltpu.VMEM((bm, D_MODEL), BF16)] + ([] if r == 1 else [_lane_tile_scratch(bm, F32)]),
        compiler_params=_params("parallel", "parallel"),
        name=f"qkv_r{r}",
    )(x, w, w, w)


def _attn_body(q_ref, kp_ref, kc_ref, kn_ref, vp_ref, vc_ref, vn_ref, bias_ref,
               o_ref, lse_ref, vbuf, *, tile, n_tiles):
    i = pl.program_id(2)
    n_seqs = q_ref.shape[0]

    def key_window(u, t, cols):
        lo, hi = t * QT - HALF, t * QT + QT + HALF
        parts = [kp_ref[u, :, cols]] if lo < 0 else []
        parts.append(kc_ref[u, max(lo, 0):min(hi, tile), cols])
        if hi > tile:
            parts.append(kn_ref[u, :, cols])
        return parts[0] if len(parts) == 1 else jnp.concatenate(parts, axis=0)

    def fill_values(u, lo, hi):
        segments = ((0, HALF, vp_ref), (HALF, HALF + tile, vc_ref), (HALF + tile, tile + 2 * HALF, vn_ref))
        for start, stop, ref in segments:
            a, b = max(lo, start), min(hi, stop)
            if a < b:
                for hp in range(N_HEADS // 2):
                    vbuf[u, a:b, 2 * hp * LANES:(2 * hp + 1) * LANES] = (
                        ref[u, a - start:b - start, hp * LANES:(hp + 1) * LANES])

    @pl.when((pl.program_id(0) == 0) & (pl.program_id(1) == 0) & (i == 0))
    def _():
        for u in range(n_seqs):
            for hp in range(N_HEADS // 2):
                vbuf[u, :, (2 * hp + 1) * LANES:(2 * hp + 2) * LANES] = jnp.ones((tile + 2 * HALF, LANES), BF16)

    lane = lax.broadcasted_iota(jnp.int32, (QT, LANES), 1)
    first_head = lane < HEAD_DIM
    n_sub = tile // QT
    at_first_tile = (i == 0).astype(jnp.int32)
    at_last_tile = (i == n_tiles - 1).astype(jnp.int32)

    for u in range(n_seqs):
        for t in range(n_sub):
            rows = slice(t * QT, (t + 1) * QT)
            keys = slice(t * QT, t * QT + KW)
            fill_values(u, 0 if t == 0 else keys.stop - QT, keys.stop)
            variant = (at_first_tile if t == 0 else 0) + 2 * (at_last_tile if t == n_sub - 1 else 0)
            lse_tile = jnp.zeros((QT, LANES), F32)
            for hp in range(N_HEADS // 2):
                cols = slice(hp * LANES, (hp + 1) * LANES)
                q2 = q_ref[u, rows, cols]
                kw = key_window(u, t, cols)
                v1 = vbuf[u, keys, 2 * hp * LANES:(2 * hp + 2) * LANES]
                zero = jnp.zeros_like(q2)
                q_pair = jnp.concatenate([jnp.where(first_head, q2, zero), jnp.where(first_head, zero, q2)],
                                         axis=0)
                s = lax.dot_general(q_pair, kw, (((1,), (1,)), ((), ())),
                                    preferred_element_type=F32) + bias_ref[variant, hp]
                m = jnp.max(s, axis=-1, keepdims=True)
                p = jnp.exp2(s - m).astype(BF16)
                pv = jnp.dot(p, v1, preferred_element_type=F32)
                out = jnp.where(first_head, pv[:QT, :LANES], pv[QT:, :LANES])
                den = jnp.where(first_head, pv[:QT, LANES:], pv[QT:, LANES:])
                o_ref[u, rows, cols] = (out * (1.0 / den)).astype(BF16)
                lse = jnp.where(first_head, m[:QT], m[QT:]) * LN2 + jnp.log(den)
                lse_tile = jnp.where((lane & (HEAD_DIM - 1)) == hp, lse, lse_tile)
            lse_ref[u, rows, :] = lse_tile


def _attention_group(qkv, bias_t):
    n_seq, r, sub_len, _ = qkv.shape
    tile = min(ATT_ROWS, sub_len)
    assert sub_len % tile == 0 and tile % QT == 0
    n_tiles = sub_len // tile
    cs = min(r, ATT_ROWS // tile)
    hb = tile // HALF
    last_hb = sub_len // HALF - 1

    def cur(which, width=D_MODEL):
        return pl.BlockSpec((None, cs, tile, width), lambda b, c, i: (b, c, i, which))

    def prev(which):
        return pl.BlockSpec((None, cs, HALF, D_MODEL),
                            lambda b, c, i: (b, c, jnp.maximum(i * hb - 1, 0), which))

    def nxt(which):
        return pl.BlockSpec((None, cs, HALF, D_MODEL),
                            lambda b, c, i: (b, c, jnp.minimum((i + 1) * hb, last_hb), which))

    return pl.pallas_call(
        functools.partial(_attn_body, tile=tile, n_tiles=n_tiles),
        grid=(n_seq, r // cs, n_tiles),
        in_specs=[cur(0), prev(1), cur(1), nxt(1), prev(2), cur(2), nxt(2),
                  _const_spec((4, N_HEADS // 2, 2 * QT, KW))],
        out_specs=[cur(0), cur(0, LANES)],
        out_shape=[jax.ShapeDtypeStruct((n_seq, r, sub_len, D_MODEL), BF16),
                   jax.ShapeDtypeStruct((n_seq, r, sub_len, LANES), F32)],
        scratch_shapes=[pltpu.VMEM((cs, tile + 2 * HALF, 2 * D_MODEL), BF16)],
        compiler_params=_params("arbitrary", "arbitrary", "arbitrary"),
        name=f"attn_r{r}",
    )(qkv, qkv, qkv, qkv, qkv, qkv, qkv, bias_t)


def _rel_bucket_np(rel):
    nb = N_BUCKETS // 2
    max_exact = nb // 2
    ret = np.where(rel > 0, nb, 0)
    n = np.abs(rel)
    nf = np.maximum(n, 1).astype(np.float32)
    large = max_exact + (np.log(nf / np.float32(max_exact)) / np.float32(math.log(MAX_DISTANCE / max_exact))
                         * np.float32(nb - max_exact)).astype(np.int32)
    large = np.minimum(large, nb - 1)
    return ret + np.where(n < max_exact, n, large)


def _band_bias(rel_bias):
    kk = np.arange(KW)[None, :]
    tables = []
    for g, (_, r) in enumerate(DIL_GROUPS):
        onehot = np.zeros((N_BUCKETS, 2 * HALF + 1), np.float32)
        onehot[_rel_bucket_np(np.arange(-HALF, HALF + 1) * r), np.arange(2 * HALF + 1)] = 1.0
        vals = jnp.dot(rel_bias[:, g * N_HEADS:(g + 1) * N_HEADS].astype(F32).T, onehot,
                       precision=lax.Precision.HIGHEST)
        period = jnp.concatenate([vals, jnp.full((N_HEADS, KW - 2 * HALF), NEG_INF, F32)], axis=1)
        band = jnp.broadcast_to(period[:, None, :], (N_HEADS, QT, KW + 1)).reshape(N_HEADS, QT * (KW + 1))
        band = band[:, :QT * KW].reshape(N_HEADS, QT, KW)
        variants = []
        for v in range(4):
            ok = np.ones((1, KW), bool)
            if v & 1:
                ok &= kk >= HALF
            if v & 2:
                ok &= kk < KW - HALF
            variants.append(jnp.where(ok[None], band, NEG_INF))
        tables.append((jnp.stack(variants) * LOG2E).reshape(4, N_HEADS // 2, 2 * QT, KW))
    return tables


def _split3(x):
    hi = x.astype(BF16)
    rest = x - hi.astype(F32)
    mid = rest.astype(BF16)
    lo = (rest - mid.astype(F32)).astype(BF16)
    return hi, mid, lo


def _merge_proj_body(o0_ref, o1_ref, o2_ref, l0_ref, l1_ref, l2_ref, x_ref, p1_ref, p2_ref,
                     e_ref, w_ref, g_ref, b_ref, out_ref, *, bm):
    def natural(ref, p_ref, k, width, exact_f32):
        n = bm // ref.shape[0]
        v = ref[:, k * n:(k + 1) * n, :].reshape(bm, width)
        if p_ref is None:
            return v.astype(F32)
        perm = p_ref[...]
        if not exact_f32:
            return jnp.dot(perm, v, preferred_element_type=F32)
        moved = jnp.dot(perm, jnp.concatenate(_split3(v), axis=1), preferred_element_type=F32)
        return moved[:, :width] + moved[:, width:2 * width] + moved[:, 2 * width:]

    perms = (None, p1_ref, p2_ref)
    for k in range(x_ref.shape[0] // bm):
        lses = [natural(ref, p, k, LANES, True) for ref, p in zip((l0_ref, l1_ref, l2_ref), perms)]
        top = jnp.maximum(jnp.maximum(lses[0], lses[1]), lses[2])
        es = [jnp.exp(l - top) for l in lses]
        inv = 1.0 / (es[0] + es[1] + es[2])
        o = jnp.zeros((bm, D_MODEL), F32)
        for ref, p, e in zip((o0_ref, o1_ref, o2_ref), perms, es):
            hi, mid, _ = _split3(e * inv)
            wide = jnp.dot(jnp.concatenate([hi, mid], axis=1), e_ref[...], preferred_element_type=F32)
            o = o + wide * natural(ref, p, k, D_MODEL, False)
        h = jnp.dot(o.astype(BF16), w_ref[...], preferred_element_type=F32)
        rows = slice(k * bm, (k + 1) * bm)
        out_ref[rows, :] = _layer_norm(ALPHA * x_ref[rows, :] + h, g_ref[...], b_ref[...])


def _perm_matrix(bm, r):
    n = bm // r
    p = np.arange(bm)
    m = np.zeros((bm, bm), np.float32)
    m[p, (p % r) * n + p // r] = 1.0
    return jnp.asarray(m, dtype=BF16)


def _merge_proj(outs, lses, x, w, g, b, mixer_layer, layer):
    n_seq, seq_len, _ = x.shape
    bm = MXU_DIM
    step = 4 * bm
    specs = []
    for width in (D_MODEL, LANES):
        for _, r in DIL_GROUPS:
            specs.append(pl.BlockSpec((None, r, step // r, width), lambda b_, i: (b_, 0, i, 0)))
    row = pl.BlockSpec((None, step, D_MODEL), lambda b_, i: (b_, i, 0))
    expand = np.zeros((2 * LANES, D_MODEL), np.float32)
    for h in range(N_HEADS):
        lane = HEAD_DIM * (h % 2) + h // 2
        expand[[lane, LANES + lane], h * HEAD_DIM:(h + 1) * HEAD_DIM] = 1.0
    vec = _layer_spec((1, D_MODEL), layer)
    return pl.pallas_call(
        functools.partial(_merge_proj_body, bm=bm),
        grid=(n_seq, seq_len // step),
        in_specs=specs + [row, _const_spec((bm, bm)), _const_spec((bm, bm)),
                          _const_spec((2 * LANES, D_MODEL)), _layer_spec((D_MODEL, D_MODEL), mixer_layer),
                          vec, vec],
        out_specs=row,
        out_shape=jax.ShapeDtypeStruct((n_seq, seq_len, D_MODEL), F32),
        compiler_params=_params("parallel", "parallel"),
        name="merge_proj_ln",
    )(*outs, *lses, x, _perm_matrix(bm, DIL_GROUPS[1][1]), _perm_matrix(bm, DIL_GROUPS[2][1]),
      jnp.asarray(expand, dtype=BF16), w, g, b)


def _ffn_passes(xs, out_ref, wg_ref, wu_ref, wd_ref, g, b):
    n_chunks = D_FF // FF_CHUNK
    xb = [x.astype(BF16) for x in xs]
    acc = [jnp.zeros(x.shape, F32) for x in xs]

    def chunk(k, c):
        cols = slice(c * FF_CHUNK, (c + 1) * FF_CHUNK)
        gate = jnp.dot(xb[k], wg_ref[:, cols], preferred_element_type=F32)
        up = jnp.dot(xb[k], wu_ref[:, cols], preferred_element_type=F32)
        h = (gate * jax.nn.sigmoid(gate) * up).astype(BF16)
        acc[k] = acc[k] + jnp.dot(h, wd_ref[cols, :], preferred_element_type=F32)

    for step in range(n_chunks + FF_LAG * (len(xs) - 1)):
        for k in range(len(xs)):
            c = step - k * FF_LAG
            if 0 <= c < n_chunks:
                chunk(k, c)
            if c == n_chunks - 1:
                out_ref[k * FF_ROWS:(k + 1) * FF_ROWS, :] = _layer_norm(ALPHA * xs[k] + acc[k], g, b)


def _ffn_body(x_ref, wg_ref, wu_ref, wd_ref, g_ref, b_ref, out_ref):
    xs = [x_ref[k * FF_ROWS:(k + 1) * FF_ROWS, :] for k in range(x_ref.shape[0] // FF_ROWS)]
    _ffn_passes(xs, out_ref, wg_ref, wu_ref, wd_ref, g_ref[...], b_ref[...])


def _ffn_digit_major_body(x_ref, wg_ref, wu_ref, wd_ref, g_ref, b_ref, out_ref, x2, *, n1, slabs):
    _stage_lane_tiles(x2, x_ref[...].reshape(n1 * slabs, D_MODEL))
    per_pass = FF_ROWS // n1
    xs = [jnp.concatenate([_strided_rows(x2, t, n1, slabs)
                           for t in range(h * per_pass, (h + 1) * per_pass)], axis=0)
          for h in range(slabs // per_pass)]
    _ffn_passes(xs, out_ref, wg_ref, wu_ref, wd_ref, g_ref[...], b_ref[...])


def _ffn(x, wg, wu, wd, g, b, layer, digit_major):
    vec = _layer_spec((1, D_MODEL), layer)
    weights = [_layer_spec((D_MODEL, D_FF), layer), _layer_spec((D_MODEL, D_FF), layer),
               _layer_spec((D_FF, D_MODEL), layer)]
    if digit_major:
        n_seq, n1, n2, _ = x.shape
        slabs = DFT_SLAB
        bm = n1 * slabs
        body = functools.partial(_ffn_digit_major_body, n1=n1, slabs=slabs)
        x_spec = pl.BlockSpec((None, n1, slabs, D_MODEL), lambda b_, i: (b_, 0, i, 0))
        seq_len = n1 * n2
        scratch = [_lane_tile_scratch(bm, F32)]
    else:
        n_seq, seq_len, _ = x.shape
        bm = 2 * FF_ROWS
        body = _ffn_body
        x_spec = pl.BlockSpec((None, bm, D_MODEL), lambda b_, i: (b_, i, 0))
        scratch = []
    return pl.pallas_call(
        body,
        grid=(n_seq, seq_len // bm),
        in_specs=[x_spec] + weights + [vec, vec],
        out_specs=pl.BlockSpec((None, bm, D_MODEL), lambda b_, i: (b_, i, 0)),
        out_shape=jax.ShapeDtypeStruct((n_seq, seq_len, D_MODEL), F32),
        scratch_shapes=scratch,
        compiler_params=_params("parallel", "parallel"),
        name="swiglu_ln_digit_major" if digit_major else "swiglu_ln",
    )(x, wg, wu, wd, g, b)


def _dft_factors(seq_len):
    bits = seq_len.bit_length() - 1
    assert 1 << bits == seq_len
    n1 = 1 << ((bits + 1) // 2)
    return n1, seq_len // n1


def _dft_tables(seq_len):
    n1, n2 = _dft_factors(seq_len)
    c = np.arange(FGROUP)
    ang = 2 * np.pi * ((c[:, None] * c[None, :]) % FGROUP) / FGROUP
    wc = np.concatenate([np.cos(ang), np.sin(ang)], axis=1) / math.sqrt(FGROUP)
    k1 = np.arange(n1)
    ang = 2 * np.pi * ((k1[:, None] * k1[None, :]) % n1) / n1
    w1 = np.block([[np.cos(ang), -np.sin(ang)], [np.sin(ang), np.cos(ang)]])
    k = k1[:, None, None] + n1 * np.arange(n2)[None, :, None]
    ang = 2 * np.pi * ((k * np.arange(n2)[None, None, :]) % seq_len) / seq_len
    m2 = np.concatenate([np.cos(ang), -np.sin(ang)], axis=2) / math.sqrt(seq_len)
    return tuple(jnp.asarray(t, dtype=BF16) for t in (wc, w1, m2))


def _bf16_bits(x):
    u = lax.bitcast_convert_type(x, U32)
    return u + jnp.uint32(0x7FFF) + ((u >> 16) & jnp.uint32(1))


def _pack_complex(re, im):
    return (_bf16_bits(re) >> 16) | (_bf16_bits(im) & jnp.uint32(0xFFFF0000))


def _unpack_complex(word):
    re = lax.bitcast_convert_type(word << 16, F32)
    im = lax.bitcast_convert_type(word & jnp.uint32(0xFFFF0000), F32)
    return re.astype(BF16), im.astype(BF16)


def _dft_front_body(x_ref, wc_ref, w1_ref, a_ref, x2, xb, zbuf, *, n1):
    _stage_lane_tiles(x2, x_ref[...].reshape(n1 * DFT_SLAB, D_MODEL))
    for t in range(DFT_SLAB):
        xb[t * n1:(t + 1) * n1, :] = _strided_rows(x2, t, n1, DFT_SLAB).astype(BF16)
    for g in range(N_FGROUPS):
        cols = slice(g * FGROUP, (g + 1) * FGROUP)
        y = jnp.dot(xb[:, cols], wc_ref[...], preferred_element_type=F32)
        for t in range(DFT_SLAB):
            rows = slice(t * n1, (t + 1) * n1)
            zbuf[t, 0:n1, cols] = y[rows, :FGROUP].astype(BF16)
            zbuf[t, n1:, cols] = y[rows, FGROUP:].astype(BF16)
    for t in range(DFT_SLAB):
        a = jnp.dot(w1_ref[...], zbuf[t], preferred_element_type=F32)
        a_ref[t] = _pack_complex(a[:n1], a[n1:])


def _dft_back_body(a_ref, x_ref, m_ref, w_ref, bo_ref, g_ref, b_ref, out_ref, a2, x2, fbuf,
                   *, n2, kb):
    _stage_lane_tiles(a2, a_ref[...].reshape(n2 * kb, D_MODEL))
    _stage_lane_tiles(x2, x_ref[...].reshape(n2 * kb, D_MODEL))
    half = kb // 2
    for q0 in range(0, kb, half):
        for q in range(q0, q0 + half):
            re, im = _unpack_complex(_strided_rows(a2, q, n2, kb))
            f = jnp.dot(m_ref[q], jnp.concatenate([re, im], axis=0), preferred_element_type=F32)
            fbuf[q * n2:(q + 1) * n2, :] = f.astype(BF16)
        h = jnp.dot(fbuf[q0 * n2:(q0 + half) * n2, :], w_ref[...], preferred_element_type=F32) + bo_ref[...]
        for q in range(q0, q0 + half):
            y = ALPHA * _strided_rows(x2, q, n2, kb) + h[(q - q0) * n2:(q - q0 + 1) * n2]
            out_ref[q] = _layer_norm(y, g_ref[...], b_ref[...])


def _fourier_mixer(x, w, bo, g, b, mixer_layer, layer):
    n_seq, seq_len, _ = x.shape
    n1, n2 = _dft_factors(seq_len)
    wc, w1, m2 = _dft_tables(seq_len)
    sb = DFT_SLAB
    a = pl.pallas_call(
        functools.partial(_dft_front_body, n1=n1),
        grid=(n_seq, n2 // sb),
        in_specs=[pl.BlockSpec((None, n1, sb, D_MODEL), lambda b_, j: (b_, 0, j, 0)),
                  _const_spec((FGROUP, 2 * FGROUP)), _const_spec((2 * n1, 2 * n1))],
        out_specs=pl.BlockSpec((None, sb, n1, D_MODEL), lambda b_, j: (b_, j, 0, 0)),
        out_shape=jax.ShapeDtypeStruct((n_seq, n2, n1, D_MODEL), U32),
        scratch_shapes=[_lane_tile_scratch(sb * n1, F32),
                        pltpu.VMEM((sb * n1, D_MODEL), BF16),
                        pltpu.VMEM((sb, 2 * n1, D_MODEL), BF16)],
        compiler_params=_params("parallel", "parallel"),
        name="dft_front",
    )(x.reshape(n_seq, n1, n2, D_MODEL), wc, w1)
    kb = DFT_SLAB
    slab = pl.BlockSpec((None, n2, kb, D_MODEL), lambda b_, k: (b_, 0, k, 0))
    m_spec = pl.BlockSpec((kb, n2, 2 * n2), lambda b_, k: (k, 0, 0))
    vec = _layer_spec((1, D_MODEL), layer)
    return pl.pallas_call(
        functools.partial(_dft_back_body, n2=n2, kb=kb),
        grid=(n_seq, n1 // kb),
        in_specs=[slab, slab, m_spec, _layer_spec((D_MODEL, D_MODEL), mixer_layer),
                  _layer_spec((1, D_MODEL), mixer_layer), vec, vec],
        out_specs=pl.BlockSpec((None, kb, n2, D_MODEL), lambda b_, k: (b_, k, 0, 0)),
        out_shape=jax.ShapeDtypeStruct((n_seq, n1, n2, D_MODEL), F32),
        scratch_shapes=[_lane_tile_scratch(kb * n2, U32), _lane_tile_scratch(kb * n2, F32),
                        pltpu.VMEM((kb * n2, D_MODEL), BF16)],
        compiler_params=_params("parallel", "parallel"),
        name="dft_back_proj_ln",
    )(a, x.reshape(n_seq, n2, n1, D_MODEL), m2, w, bo, g, b)


def _trunk(x, p):
    for i in range(DEPTH):
        li = i // 2
        mixer_a = i % 2 == 0
        if mixer_a:
            outs, lses = [], []
            for g, (_, r) in enumerate(DIL_GROUPS):
                o, lse = _attention_group(_qkv_group(x, p["w_qkv"], li, g, r), p["band_bias"][g])
                outs.append(o)
                lses.append(lse)
            x = _merge_proj(outs, lses, x, p["w_o_a"], p["ln1_g"], p["ln1_b"], li, i)
        else:
            x = _fourier_mixer(x, p["w_o_b"], p["b_o_b"], p["ln1_g"], p["ln1_b"], li, i)
        x = _ffn(x, p["w_gate"], p["w_up"], p["w_down"], p["ln2_g"], p["ln2_b"], i,
                 digit_major=not mixer_a)
    return x


def kernel(x_prompt, x_sample, rel_bias, w_qkv_a, w_o_a, w_o_b, b_o_b, w_gate, w_up, w_down,
           ln1_g, ln1_b, ln2_g, ln2_b):
    q_cols = w_qkv_a.shape[-1] // 3
    col_scale = jnp.concatenate([jnp.full((q_cols,), LOG2E * HEAD_DIM ** -0.5, F32),
                                 jnp.ones((2 * q_cols,), F32)])
    vec = lambda a: a.astype(F32)[:, None, :]
    p = {
        "w_qkv": (w_qkv_a * col_scale).astype(BF16),
        "w_o_a": w_o_a.astype(BF16),
        "w_o_b": w_o_b.astype(BF16),
        "b_o_b": vec(b_o_b),
        "w_gate": w_gate.astype(BF16),
        "w_up": w_up.astype(BF16),
        "w_down": w_down.astype(BF16),
        "ln1_g": vec(ln1_g), "ln1_b": vec(ln1_b), "ln2_g": vec(ln2_g), "ln2_b": vec(ln2_b),
        "band_bias": _band_bias(rel_bias),
    }
    return _trunk(x_prompt, p), _trunk(x_sample, p)
```
